```python
import math
import jax, jax.numpy as jnp
from jax import lax
import numpy as np

D_MODEL = 4096
BATCH = 1
SEQ = 16384
DEPTH = 4

N_A_LAYERS = DEPTH // 2
N_B_LAYERS = DEPTH - N_A_LAYERS
EPS = 1e-6

FFN_HIDDEN = ((8 * D_MODEL // 3 + 255) // 256) * 256

GDN_HEADS = D_MODEL // 128
GDN_DK = 128
GDN_DV = 128
GDN_CONV = 4
GDN_CHUNK = 64
GDN_QK = GDN_HEADS * GDN_DK
GDN_V = GDN_HEADS * GDN_DV
GDN_CONV_CH = 2 * GDN_QK + GDN_V
GDN_IN = 2 * GDN_QK + 2 * GDN_V + 2 * GDN_HEADS

DIL_GROUPS = ((128, 1), (512, 4), (2048, 16))
N_DIL = len(DIL_GROUPS)
DIL_HEADS = D_MODEL // 256
DIL_KV_HEADS = DIL_HEADS // 4
DIL_HD = 128
DIL_BLOCK = 128
DIL_QW = DIL_HEADS * DIL_HD
DIL_KVW = DIL_KV_HEADS * DIL_HD

kernel_name = "yoco_gdn_dilated_hybrid"


def _rmsnorm(x, w):
    xf = x.astype(jnp.float32)
    y = xf * lax.rsqrt(jnp.mean(xf * xf, axis=-1, keepdims=True) + EPS)
    return (y * w.astype(jnp.float32)).astype(x.dtype)


def _l2norm(x):
    xf = x.astype(jnp.float32)
    return xf * lax.rsqrt(jnp.sum(xf * xf, axis=-1, keepdims=True) + EPS)


def _causal_dwconv(x, w):
    k_len, ch = w.shape
    return lax.conv_general_dilated(x, w[:, None, :], window_strides=(1,), padding=[(k_len - 1, 0)],
                                    dimension_numbers=('NWC', 'WIO', 'NWC'), feature_group_count=ch)


def _swiglu(h, w_gate_up, w_down):
    gu = h @ w_gate_up
    return (jax.nn.silu(gu[..., :FFN_HIDDEN]) * gu[..., FFN_HIDDEN:]) @ w_down


def _gated_delta_rule(q, k, v, g, beta):
    B, S, H, DK = q.shape
    DV = v.shape[-1]
    C = GDN_CHUNK
    N = S // C

    def chunks(a):
        a = a.astype(jnp.float32).reshape(B, N, C, H, *a.shape[3:])
        return jnp.moveaxis(a, 3, 1)

    q, k, v, g, beta = chunks(q), chunks(k), chunks(v), chunks(g), chunks(beta)
    gc = jnp.cumsum(g, axis=-1)
    idx = jnp.arange(C)
    causal = idx[:, None] >= idx[None, :]
    strict = idx[:, None] > idx[None, :]
    decay = jnp.where(causal, jnp.exp(jnp.where(causal, gc[..., :, None] - gc[..., None, :], 0.0)), 0.0)
    kb = k * beta[..., None]
    a_mat = jnp.where(strict, jnp.einsum('bhnid,bhnjd->bhnij', kb, k) * decay, 0.0)
    eye = jnp.eye(C, dtype=jnp.float32)
    t_mat = lax.linalg.triangular_solve(a_mat, jnp.broadcast_to(eye, a_mat.shape), left_side=True,
                                        lower=True, unit_diagonal=True)
    u = jnp.einsum('bhnij,bhnjd->bhnid', t_mat, v * beta[..., None])
    w = jnp.einsum('bhnij,bhnjd->bhnid', t_mat, kb * jnp.exp(gc)[..., None])
    attn = jnp.where(causal, jnp.einsum('bhnid,bhnjd->bhnij', q, k) * decay, 0.0)
    q_dec = q * jnp.exp(gc)[..., None]
    k_dec = k * jnp.exp(gc[..., -1:] - gc)[..., None]
    g_last = jnp.exp(gc[..., -1])

    def step(state, xs):
        u_n, w_n, qd_n, kd_n, a_n, gl_n = xs
        v_new = u_n - jnp.einsum('bhcd,bhde->bhce', w_n, state)
        o_n = jnp.einsum('bhcd,bhde->bhce', qd_n, state) + jnp.einsum('bhij,bhje->bhie', a_n, v_new)
        state = state * gl_n[..., None, None] + jnp.einsum('bhcd,bhce->bhde', kd_n, v_new)
        return state, o_n

    xs = tuple(jnp.moveaxis(a, 2, 0) for a in (u, w, q_dec, k_dec, attn, g_last))
    state0 = jnp.zeros((B, H, DK, DV), jnp.float32)
    _, o = lax.scan(step, state0, xs)
    return jnp.transpose(o, (1, 0, 3, 2, 4)).reshape(B, S, H, DV)


def _gdn_mixer(h, w_in, conv_w, a_log, dt_bias, out_norm, w_out):
    B, S, _ = h.shape
    proj = h @ w_in
    qkv = jax.nn.silu(_causal_dwconv(proj[..., :GDN_CONV_CH], conv_w))
    q = _l2norm(qkv[..., :GDN_QK].reshape(B, S, GDN_HEADS, GDN_DK)) * (GDN_DK ** -0.5)
    k = _l2norm(qkv[..., GDN_QK:2 * GDN_QK].reshape(B, S, GDN_HEADS, GDN_DK))
    v = qkv[..., 2 * GDN_QK:].reshape(B, S, GDN_HEADS, GDN_DV)
    z = proj[..., GDN_CONV_CH:GDN_CONV_CH + GDN_V].reshape(B, S, GDN_HEADS, GDN_DV)
    b = proj[..., GDN_CONV_CH + GDN_V:GDN_CONV_CH + GDN_V + GDN_HEADS].astype(jnp.float32)
    a = proj[..., GDN_CONV_CH + GDN_V + GDN_HEADS:].astype(jnp.float32)
    beta = jax.nn.sigmoid(b)
    g = -jnp.exp(a_log.astype(jnp.float32)) * jax.nn.softplus(a + dt_bias.astype(jnp.float32))
    o = _gated_delta_rule(q, k, v, g, beta)
    o = _rmsnorm(o, out_norm) * jax.nn.silu(z.astype(jnp.float32))
    return o.reshape(B, S, GDN_V).astype(h.dtype) @ w_out


def _dilated_window_attention(q, k, v, dilation, span):
    B, S, H, hd = q.shape
    G = k.shape[2]
    R = H // G
    L = S // dilation
    n_blk = -(-L // DIL_BLOCK)
    Lp = n_blk * DIL_BLOCK

    def split(a):
        a = jnp.moveaxis(a.reshape(B, L, dilation, *a.shape[2:]), 2, 1)
        a = jnp.pad(a, [(0, 0), (0, 0), (0, Lp - L)] + [(0, 0)] * (a.ndim - 3))
        return a.reshape(B, dilation, n_blk, DIL_BLOCK, *a.shape[3:])

    def with_prev(a):
        prev = jnp.pad(a, [(0, 0), (0, 0), (1, 0)] + [(0, 0)] * (a.ndim - 3))[:, :, :-1]
        return jnp.concatenate([prev, a], axis=3)

    qb = split(q).reshape(B, dilation, n_blk, DIL_BLOCK, G, R, hd)
    kw = with_prev(split(k))
    vw = with_prev(split(v)).astype(jnp.float32)
    s = jnp.einsum('bpnqgrd,bpnkgd->bpngrqk', qb, kw, preferred_element_type=jnp.float32)
    qi = jnp.arange(DIL_BLOCK)[:, None] + DIL_BLOCK
    ki = jnp.arange(2 * DIL_BLOCK)[None, :]
    dist = qi - ki
    blk = jnp.arange(n_blk)[:, None, None]
    valid = (dist >= 0) & (dist <= span) & (blk * DIL_BLOCK + ki - DIL_BLOCK >= 0)
    s = jnp.where(valid[:, None, None], s, -jnp.inf)
    m = jnp.max(s, axis=-1, keepdims=True)
    e = jnp.exp(s - m)
    den = jnp.sum(e, axis=-1, keepdims=True)
    o = jnp.einsum('bpngrqk,bpnkgd->bpngrqd', e, vw) / den
    lse = (m + jnp.log(den))[..., 0]
    o = o.transpose(0, 1, 2, 5, 3, 4, 6).reshape(B, dilation, Lp, H, hd)[:, :, :L]
    o = jnp.moveaxis(o, 1, 2).reshape(B, S, H, hd)
    lse = lse.transpose(0, 1, 2, 5, 3, 4).reshape(B, dilation, Lp, H)[:, :, :L]
    lse = jnp.moveaxis(lse, 1, 2).reshape(B, S, H)
    return o, lse


def _shared_kv(x, kv_norm, w_kv, k_norm):
    B, S, _ = x.shape
    kv = (_rmsnorm(x, kv_norm) @ w_kv).reshape(B, S, N_DIL, 2, DIL_KV_HEADS, DIL_HD)
    ks = _rmsnorm(kv[:, :, :, 0], k_norm[:, None, :])
    vs = kv[:, :, :, 1]
    return ks, vs


def _dilated_mixer(h, ks, vs, w_q, q_norm, w_out):
    B, S, _ = h.shape
    q = (h @ w_q).reshape(B, S, N_DIL, DIL_HEADS, DIL_HD)
    q = _rmsnorm(q, q_norm[:, None, :]) * (DIL_HD ** -0.5)
    outs, lses = [], []
    for gi, (window, dilation) in enumerate(DIL_GROUPS):
        o, lse = _dilated_window_attention(q[:, :, gi], ks[:, :, gi], vs[:, :, gi], dilation, window // dilation)
        outs.append(o)
        lses.append(lse)
    wts = jax.nn.softmax(jnp.stack(lses), axis=0)
    o = jnp.sum(wts[..., None] * jnp.stack(outs), axis=0)
    return o.reshape(B, S, DIL_QW).astype(h.dtype) @ w_out


def setup_inputs(seed: int = 0) -> dict:
    key = jax.random.key(seed)
    ks = jax.random.split(key, 20)
    f32 = jnp.float32

    def nrm(k, shape, fan_in):
        return jax.random.normal(k, shape, f32) * (fan_in ** -0.5)

    def gain(k, shape):
        return 1.0 + 0.02 * jax.random.normal(k, shape, f32)

    x = jax.random.normal(ks[0], (BATCH, SEQ, D_MODEL), f32)
    a_attn_norm = gain(ks[1], (N_A_LAYERS, D_MODEL))
    a_w_in = nrm(ks[2], (N_A_LAYERS, D_MODEL, GDN_IN), D_MODEL)
    a_conv_w = nrm(ks[3], (N_A_LAYERS, GDN_CONV, GDN_CONV_CH), GDN_CONV)
    a_a_log = jnp.log(jax.random.uniform(ks[4], (N_A_LAYERS, GDN_HEADS), f32, 1.0, 16.0))
    dt = jnp.exp(jax.random.uniform(ks[5], (N_A_LAYERS, GDN_HEADS), f32, math.log(1e-3), math.log(1e-1)))
    a_dt_bias = jnp.log(jnp.expm1(dt))
    a_out_norm = gain(ks[6], (N_A_LAYERS, GDN_DV))
    a_w_out = nrm(ks[7], (N_A_LAYERS, GDN_V, D_MODEL), GDN_V)
    kv_norm = gain(ks[8], (D_MODEL,))
    w_kv = nrm(ks[9], (D_MODEL, N_DIL * 2 * DIL_KVW), D_MODEL)
    k_norm = gain(ks[10], (N_DIL, DIL_HD))
    b_attn_norm = gain(ks[11], (N_B_LAYERS, D_MODEL))
    b_w_q = nrm(ks[12], (N_B_LAYERS, D_MODEL, N_DIL * DIL_QW), D_MODEL)
    b_q_norm = gain(ks[13], (N_B_LAYERS, N_DIL, DIL_HD))
    b_w_out = nrm(ks[14], (N_B_LAYERS, DIL_QW, D_MODEL), DIL_QW)
    ffn_norm = gain(ks[15], (DEPTH, D_MODEL))
    ffn_w_gate_up = nrm(ks[16], (DEPTH, D_MODEL, 2 * FFN_HIDDEN), D_MODEL)
    ffn_w_down = nrm(ks[17], (DEPTH, FFN_HIDDEN, D_MODEL), FFN_HIDDEN)
    return {"x": x, "a_attn_norm": a_attn_norm, "a_w_in": a_w_in, "a_conv_w": a_conv_w,
            "a_a_log": a_a_log, "a_dt_bias": a_dt_bias, "a_out_norm": a_out_norm, "a_w_out": a_w_out,
            "kv_norm": kv_norm, "w_kv": w_kv, "k_norm": k_norm, "b_attn_norm": b_attn_norm,
            "b_w_q": b_w_q, "b_q_norm": b_q_norm, "b_w_out": b_w_out, "ffn_norm": ffn_norm,
            "ffn_w_gate_up": ffn_w_gate_up, "ffn_w_down": ffn_w_down}


def reference(x, a_attn_norm, a_w_in, a_conv_w, a_a_log, a_dt_bias, a_out_norm, a_w_out,
              kv_norm, w_kv, k_norm, b_attn_norm, b_w_q, b_q_norm, b_w_out,
              ffn_norm, ffn_w_gate_up, ffn_w_down):
    shared_k, shared_v = None, None
    for layer in range(DEPTH):
        if layer < N_A_LAYERS:
            i = layer
            x = x + _gdn_mixer(_rmsnorm(x, a_attn_norm[i]), a_w_in[i], a_conv_w[i], a_a_log[i],
                               a_dt_bias[i], a_out_norm[i], a_w_out[i])
        else:
            if layer == N_A_LAYERS:
                shared_k, shared_v = _shared_kv(x, kv_norm, w_kv, k_norm)
            j = layer - N_A_LAYERS
            x = x + _dilated_mixer(_rmsnorm(x, b_attn_norm[j]), shared_k, shared_v, b_w_q[j], b_q_norm[j], b_w_out[j])
        x = x + _swiglu(_rmsnorm(x, ffn_norm[layer]), ffn_w_gate_up[layer], ffn_w_down[layer])
    return x
```

```python
import functools
import math

import jax
import jax.numpy as jnp
from jax import lax
from jax.experimental import pallas as pl
from jax.experimental.pallas import tpu as pltpu

V7X_VMEM_LIMIT_BYTES = 56 * 1024 * 1024
LANE = 128

EPS = 1e-6
GDN_CHUNK = 64
GDN_CONV = 4
DIL_GROUPS = ((128, 1), (512, 4), (2048, 16))
DIL_BLOCK = 128
HEAD_DIM = 128

MM_BM = 1024
MM_BN = 1024
SWIGLU_BN = 512
DOWN_BK_MAX = 3072
DOWN_BK_STEP = 256
NORM_BM = 256


def _pick_down_bk(k_padded):
    best = DOWN_BK_STEP
    for bk in range(DOWN_BK_STEP, DOWN_BK_MAX + 1, DOWN_BK_STEP):
        if k_padded % bk == 0:
            best = bk
    return best


def _params(*semantics):
    return pltpu.CompilerParams(dimension_semantics=semantics, vmem_limit_bytes=V7X_VMEM_LIMIT_BYTES)


def _rmsnorm_kernel(x_ref, w_ref, o_ref):
    x = x_ref[...]
    y = x * lax.rsqrt(jnp.mean(x * x, axis=-1, keepdims=True) + EPS)
    o_ref[...] = (y * w_ref[...]).astype(o_ref.dtype)


def _rmsnorm(x, w, out_dtype=jnp.bfloat16):
    s, d = x.shape
    bm = min(NORM_BM, s)
    return pl.pallas_call(
        _rmsnorm_kernel,
        out_shape=jax.ShapeDtypeStruct((s, d), out_dtype),
        grid=(s // bm,),
        in_specs=[pl.BlockSpec((bm, d), lambda i: (i, 0)), pl.BlockSpec((1, d), lambda i: (0, 0))],
        out_specs=pl.BlockSpec((bm, d), lambda i: (i, 0)),
        compiler_params=_params("parallel"),
        name="rmsnorm",
    )(x, w.reshape(1, d))


def _mm_kernel(a_ref, w_ref, o_ref):
    o_ref[...] = jnp.dot(a_ref[...], w_ref[...], preferred_element_type=jnp.float32).astype(o_ref.dtype)


def _mm_res_kernel(a_ref, w_ref, r_ref, o_ref):
    acc = jnp.dot(a_ref[...], w_ref[...], preferred_element_type=jnp.float32)
    o_ref[...] = (r_ref[...] + acc).astype(o_ref.dtype)


def _matmul(a, w, residual=None, out_dtype=jnp.float32, name="matmul"):
    m, k = a.shape
    n = w.shape[1]
    bm, bn = min(MM_BM, m), min(MM_BN, n)
    in_specs = [pl.BlockSpec((bm, k), lambda i, j: (i, 0)), pl.BlockSpec((k, bn), lambda i, j: (0, j))]
    args = [a, w]
    kern = _mm_kernel
    if residual is not None:
        in_specs.append(pl.BlockSpec((bm, bn), lambda i, j: (i, j)))
        args.append(residual)
        kern = _mm_res_kernel
    return pl.pallas_call(
        kern,
        out_shape=jax.ShapeDtypeStruct((m, n), out_dtype),
        grid=(m // bm, n // bn),
        in_specs=in_specs,
        out_specs=pl.BlockSpec((bm, bn), lambda i, j: (i, j)),
        compiler_params=_params("parallel", "parallel"),
        name=name,
    )(*args)


def _mm_headnorm_kernel(a_ref, w_ref, g_ref, o_ref, *, scale, norm_every):
    y = jnp.dot(a_ref[...], w_ref[...], preferred_element_type=jnp.float32)
    bn = y.shape[1]

    def normed():
        for c in range(bn // LANE):
            ys = y[:, c * LANE:(c + 1) * LANE]
            inv = lax.rsqrt(jnp.mean(ys * ys, axis=-1, keepdims=True) + EPS)
            o_ref[:, c * LANE:(c + 1) * LANE] = (ys * inv * g_ref[:, c * LANE:(c + 1) * LANE] * scale).astype(o_ref.dtype)

    if norm_every == 1:
        normed()
    else:
        j = pl.program_id(1)

        @pl.when(j % norm_every == 0)
        def _():
            normed()

        @pl.when(j % norm_every != 0)
        def _():
            o_ref[...] = y.astype(o_ref.dtype)


def _matmul_headnorm(a, w, gain_row, *, scale, bn, norm_every, name):
    m, k = a.shape
    n = w.shape[1]
    bm = min(MM_BM, m)
    return pl.pallas_call(
        functools.partial(_mm_headnorm_kernel, scale=scale, norm_every=norm_every),
        out_shape=jax.ShapeDtypeStruct((m, n), jnp.float32),
        grid=(m // bm, n // bn),
        in_specs=[
            pl.BlockSpec((bm, k), lambda i, j: (i, 0)),
            pl.BlockSpec((k, bn), lambda i, j: (0, j)),
            pl.BlockSpec((1, bn), lambda i, j: (0, j)),
        ],
        out_specs=pl.BlockSpec((bm, bn), lambda i, j: (i, j)),
        compiler_params=_params("parallel", "parallel"),
        name=name,
    )(a, w, gain_row)


def _swiglu_up_kernel(h_ref, wg_ref, wu_ref, o_ref, *, hidden):
    h = h_ref[...]
    g = jnp.dot(h, wg_ref[...], preferred_element_type=jnp.float32)
    u = jnp.dot(h, wu_ref[...], preferred_element_type=jnp.float32)
    act = g * jax.nn.sigmoid(g) * u
    bn = act.shape[1]
    col = pl.program_id(1) * bn + lax.broadcasted_iota(jnp.int32, act.shape, 1)
    o_ref[...] = jnp.where(col < hidden, act, 0.0).astype(o_ref.dtype)


def _swiglu_up(h, wg, wu, hidden_padded):
    m, k = h.shape
    hidden = wg.shape[1]
    bm, bn = min(MM_BM, m), SWIGLU_BN
    return pl.pallas_call(
        functools.partial(_swiglu_up_kernel, hidden=hidden),
        out_shape=jax.ShapeDtypeStruct((m, hidden_padded), jnp.bfloat16),
        grid=(m // bm, hidden_padded // bn),
        in_specs=[
            pl.BlockSpec((bm, k), lambda i, j: (i, 0)),
            pl.BlockSpec((k, bn), lambda i, j: (0, j)),
            pl.BlockSpec((k, bn), lambda i, j: (0, j)),
        ],
        out_specs=pl.BlockSpec((bm, bn), lambda i, j: (i, j)),
        compiler_params=_params("parallel", "parallel"),
        name="swiglu_up",
    )(h, wg, wu)


def _mm_acc_res_kernel(a_ref, w_ref, r_ref, o_ref, acc_ref):
    kk = pl.program_id(2)

    @pl.when(kk == 0)
    def _():
        acc_ref[...] = r_ref[...]

    acc_ref[...] += jnp.dot(a_ref[...], w_ref[...], preferred_element_type=jnp.float32)

    @pl.when(kk == pl.num_programs(2) - 1)
    def _():
        o_ref[...] = acc_ref[...]


def _matmul_ktiled_res(a, w, residual, bk, name):
    m, k = a.shape
    n = w.shape[1]
    bm, bn = min(MM_BM, m), min(MM_BN, n)
    return pl.pallas_call(
        _mm_acc_res_kernel,
        out_shape=jax.ShapeDtypeStruct((m, n), jnp.float32),
        grid=(m // bm, n // bn, k // bk),
        in_specs=[
            pl.BlockSpec((bm, bk), lambda i, j, kk: (i, kk)),
            pl.BlockSpec((bk, bn), lambda i, j, kk: (kk, j)),
            pl.BlockSpec((bm, bn), lambda i, j, kk: (i, j)),
        ],
        out_specs=pl.BlockSpec((bm, bn), lambda i, j, kk: (i, j)),
        scratch_shapes=[pltpu.VMEM((bm, bn), jnp.float32)],
        compiler_params=_params("parallel", "parallel", "arbitrary"),
        name=name,
    )(a, w, residual)


def _l2norm(x):
    return x * lax.rsqrt(jnp.sum(x * x, axis=-1, keepdims=True) + EPS)


def _causal_dwconv(x, w):
    k_len, ch = w.shape
    return lax.conv_general_dilated(x, w[:, None, :], window_strides=(1,), padding=[(k_len - 1, 0)],
                                    dimension_numbers=('NWC', 'WIO', 'NWC'), feature_group_count=ch)


def _gated_delta_rule(q, k, v, g, beta):
    B, S, H, DK = q.shape
    DV = v.shape[-1]
    C = GDN_CHUNK
    N = S // C

    def chunks(a):
        a = a.astype(jnp.float32).reshape(B, N, C, H, *a.shape[3:])
        return jnp.moveaxis(a, 3, 1)

    q, k, v, g, beta = chunks(q), chunks(k), chunks(v), chunks(g), chunks(beta)
    gc = jnp.cumsum(g, axis=-1)
    idx = jnp.arange(C)
    causal = idx[:, None] >= idx[None, :]
    strict = idx[:, None] > idx[None, :]
    decay = jnp.where(causal, jnp.exp(jnp.where(causal, gc[..., :, None] - gc[..., None, :], 0.0)), 0.0)
    kb = k * beta[..., None]
    a_mat = jnp.where(strict, jnp.einsum('bhnid,bhnjd->bhnij', kb, k) * decay, 0.0)
    eye = jnp.eye(C, dtype=jnp.float32)
    t_mat = lax.linalg.triangular_solve(a_mat, jnp.broadcast_to(eye, a_mat.shape), left_side=True,
                                        lower=True, unit_diagonal=True)
    u = jnp.einsum('bhnij,bhnjd->bhnid', t_mat, v * beta[..., None])
    w = jnp.einsum('bhnij,bhnjd->bhnid', t_mat, kb * jnp.exp(gc)[..., None])
    attn = jnp.where(causal, jnp.einsum('bhnid,bhnjd->bhnij', q, k) * decay, 0.0)
    q_dec = q * jnp.exp(gc)[..., None]
    k_dec = k * jnp.exp(gc[..., -1:] - gc)[..., None]
    g_last = jnp.exp(gc[..., -1])

    def step(state, xs):
        u_n, w_n, qd_n, kd_n, a_n, gl_n = xs
        v_new = u_n - jnp.einsum('bhcd,bhde->bhce', w_n, state)
        o_n = jnp.einsum('bhcd,bhde->bhce', qd_n, state) + jnp.einsum('bhij,bhje->bhie', a_n, v_new)
        state = state * gl_n[..., None, None] + jnp.einsum('bhcd,bhce->bhde', kd_n, v_new)
        return state, o_n

    xs = tuple(jnp.moveaxis(a, 2, 0) for a in (u, w, q_dec, k_dec, attn, g_last))
    state0 = jnp.zeros((B, H, DK, DV), jnp.float32)
    _, o = lax.scan(step, state0, xs)
    return jnp.transpose(o, (1, 0, 3, 2, 4)).reshape(B, S, H, DV)


def _dilated_window_attention(q, k, v, dilation, span):
    B, S, H, hd = q.shape
    G = k.shape[2]
    R = H // G
    L = S // dilation
    n_blk = -(-L // DIL_BLOCK)
    Lp = n_blk * DIL_BLOCK

    def split(a):
        a = jnp.moveaxis(a.reshape(B, L, dilation, *a.shape[2:]), 2, 1)
        a = jnp.pad(a, [(0, 0), (0, 0), (0, Lp - L)] + [(0, 0)] * (a.ndim - 3))
        return a.reshape(B, dilation, n_blk, DIL_BLOCK, *a.shape[3:])

    def with_prev(a):
        prev = jnp.pad(a, [(0, 0), (0, 0), (1, 0)] + [(0, 0)] * (a.ndim - 3))[:, :, :-1]
        return jnp.concatenate([prev, a], axis=3)

    qb = split(q).reshape(B, dilation, n_blk, DIL_BLOCK, G, R, hd)
    kw = with_prev(split(k))
    vw = with_prev(split(v)).astype(jnp.float32)
    s = jnp.einsum('bpnqgrd,bpnkgd->bpngrqk', qb, kw, preferred_element_type=jnp.float32)
    qi = jnp.arange(DIL_BLOCK)[:, None] + DIL_BLOCK
    ki = jnp.arange(2 * DIL_BLOCK)[None, :]
    dist = qi - ki
    blk = jnp.arange(n_blk)[:, None, None]
    valid = (dist >= 0) & (dist <= span) & (blk * DIL_BLOCK + ki - DIL_BLOCK >= 0)
    s = jnp.where(valid[:, None, None], s, -jnp.inf)
    m = jnp.max(s, axis=-1, keepdims=True)
    e = jnp.exp(s - m)
    den = jnp.sum(e, axis=-1, keepdims=True)
    o = jnp.einsum('bpngrqk,bpnkgd->bpngrqd', e, vw) / den
    lse = (m + jnp.log(den))[..., 0]
    o = o.transpose(0, 1, 2, 5, 3, 4, 6).reshape(B, dilation, Lp, H, hd)[:, :, :L]
    o = jnp.moveaxis(o, 1, 2).reshape(B, S, H, hd)
    lse = lse.transpose(0, 1, 2, 5, 3, 4).reshape(B, dilation, Lp, H)[:, :, :L]
    lse = jnp.moveaxis(lse, 1, 2).reshape(B, S, H)
    return o, lse


def _bf16(w):
    return w.astype(jnp.bfloat16)


def _gdn_layer(x, attn_norm, w_in, conv_w, a_log, dt_bias, out_norm, w_out):
    s, d = x.shape
    heads = a_log.shape[0]
    qk = heads * HEAD_DIM
    h = _rmsnorm(x, attn_norm)
    proj = _matmul(h, _bf16(w_in[:, :4 * qk]), name="gdn_in_proj")
    gates = jnp.dot(h, _bf16(w_in[:, 4 * qk:]), preferred_element_type=jnp.float32)
    qkv = jax.nn.silu(_causal_dwconv(proj[None, :, :3 * qk], conv_w))
    q = _l2norm(qkv[..., :qk].reshape(1, s, heads, HEAD_DIM)) * (HEAD_DIM ** -0.5)
    k = _l2norm(qkv[..., qk:2 * qk].reshape(1, s, heads, HEAD_DIM))
    v = qkv[..., 2 * qk:].reshape(1, s, heads, HEAD_DIM)
    z = proj[:, 3 * qk:].reshape(1, s, heads, HEAD_DIM)
    beta = jax.nn.sigmoid(gates[None, :, :heads])
    g = -jnp.exp(a_log) * jax.nn.softplus(gates[None, :, heads:] + dt_bias)
    o = _gated_delta_rule(q, k, v, g, beta)
    o = o * lax.rsqrt(jnp.mean(o * o, axis=-1, keepdims=True) + EPS) * out_norm * jax.nn.silu(z)
    return _matmul(o.reshape(s, qk).astype(jnp.bfloat16), _bf16(w_out), residual=x, name="gdn_out_proj")


def _shared_kv(x, kv_norm, w_kv, k_norm):
    n_dil = k_norm.shape[0]
    kvw = w_kv.shape[1] // (2 * n_dil)
    h = _rmsnorm(x, kv_norm)
    gain = jnp.concatenate([jnp.tile(k_norm, (1, kvw // HEAD_DIM)), jnp.ones((n_dil, kvw), jnp.float32)], axis=1).reshape(1, -1)
    return _matmul_headnorm(h, _bf16(w_kv), gain, scale=1.0, bn=kvw, norm_every=2, name="kv_proj")


def _dilated_layer(x, kv, attn_norm, w_q, q_norm, w_out):
    s, d = x.shape
    n_dil = q_norm.shape[0]
    qw = w_q.shape[1] // n_dil
    kvw = kv.shape[1] // (2 * n_dil)
    heads, kv_heads = qw // HEAD_DIM, kvw // HEAD_DIM
    h = _rmsnorm(x, attn_norm)
    gain = jnp.tile(q_norm, (1, heads)).reshape(1, -1)
    q = _matmul_headnorm(h, _bf16(w_q), gain, scale=HEAD_DIM ** -0.5, bn=min(MM_BN, qw), norm_every=1, name="q_proj")
    q = q.reshape(1, s, n_dil, heads, HEAD_DIM)
    kv5 = kv.reshape(1, s, n_dil, 2, kv_heads, HEAD_DIM)
    outs, lses = [], []
    for gi, (window, dilation) in enumerate(DIL_GROUPS):
        o, lse = _dilated_window_attention(q[:, :, gi], kv5[:, :, gi, 0], kv5[:, :, gi, 1], dilation, window // dilation)
        outs.append(o)
        lses.append(lse)
    wts = jax.nn.softmax(jnp.stack(lses), axis=0)
    o = jnp.sum(wts[..., None] * jnp.stack(outs), axis=0)
    return _matmul(o.reshape(s, qw).astype(jnp.bfloat16), _bf16(w_out), residual=x, name="attn_out_proj")


def _ffn_layer(x, norm_w, w_gate_up, w_down):
    hidden = w_down.shape[0]
    hidden_padded = -(-hidden // MM_BN) * MM_BN
    h = _rmsnorm(x, norm_w)
    act = _swiglu_up(h, _bf16(w_gate_up[:, :hidden]), _bf16(w_gate_up[:, hidden:]), hidden_padded)
    w_down_p = jnp.pad(_bf16(w_down), ((0, hidden_padded - hidden), (0, 0)))
    return _matmul_ktiled_res(act, w_down_p, x, _pick_down_bk(hidden_padded), name="ffn_down")


def kernel(x, a_attn_norm, a_w_in, a_conv_w, a_a_log, a_dt_bias, a_out_norm, a_w_out, kv_norm, w_kv, k_norm,
           b_attn_norm, b_w_q, b_q_norm, b_w_out, ffn_norm, ffn_w_gate_up, ffn_w_down):
    batch, s, d = x.shape
    assert batch == 1
    x = x.reshape(s, d)
    depth = ffn_norm.shape[0]
    n_a = a_attn_norm.shape[0]
    kv = None
    for layer in range(depth):
        if layer < n_a:
            i = layer
            x = _gdn_layer(x, a_attn_norm[i], a_w_in[i], a_conv_w[i], a_a_log[i], a_dt_bias[i], a_out_norm[i], a_w_out[i])
        else:
            if layer == n_a:
                kv = _shared_kv(x, kv_norm, w_kv, k_norm)
            j = layer - n_a
            x = _dilated_layer(x, kv, b_attn_norm[j], b_w_q[j], b_q_norm[j], b_w_out[j])
        x = _ffn_layer(x, ffn_norm[layer], ffn_w_gate_up[layer], ffn_w_down[layer])
    return x.reshape(batch, s, d)
```

```python
import functools

import jax
import jax.numpy as jnp
from jax import lax
from jax.experimental import pallas as pl
from jax.experimental.pallas import tpu as pltpu

V7X_VMEM_LIMIT_BYTES = 56 * 1024 * 1024
LANE = 128
SUBLANE = 8

EPS = 1e-6
GDN_CONV = 4
DIL_GROUPS = ((128, 1), (512, 4), (2048, 16))
DIL_BLOCK = 128
HEAD_DIM = 128

MM_BM = 1024
MM_BN = 1024
SWIGLU_BN = 512
DOWN_BK_MAX = 3072
DOWN_BK_STEP = 256
NORM_BM = 256

GDN_CHUNK = 128
GDN_TB = 1024
GDN_HEADS_PER_STEP = 2
GATES_BM = 512
ATT_SB = DIL_BLOCK * max(d for _, d in DIL_GROUPS)


def _pick_down_bk(k_padded):
    best = DOWN_BK_STEP
    for bk in range(DOWN_BK_STEP, DOWN_BK_MAX + 1, DOWN_BK_STEP):
        if k_padded % bk == 0:
            best = bk
    return best


def _params(*semantics):
    return pltpu.CompilerParams(dimension_semantics=semantics, vmem_limit_bytes=V7X_VMEM_LIMIT_BYTES)


def _dot(a, b):
    return jnp.dot(a, b, preferred_element_type=jnp.float32)


def _dot_nt(a, b):
    return lax.dot_general(a, b, (((1,), (1,)), ((), ())), preferred_element_type=jnp.float32)


def _bf(x):
    return x.astype(jnp.bfloat16)


def _rmsnorm_kernel(x_ref, w_ref, o_ref):
    x = x_ref[...]
    y = x * lax.rsqrt(jnp.mean(x * x, axis=-1, keepdims=True) + EPS)
    o_ref[...] = (y * w_ref[...]).astype(o_ref.dtype)


def _rmsnorm(x, w, out_dtype=jnp.bfloat16):
    s, d = x.shape
    bm = min(NORM_BM, s)
    return pl.pallas_call(
        _rmsnorm_kernel,
        out_shape=jax.ShapeDtypeStruct((s, d), out_dtype),
        grid=(s // bm,),
        in_specs=[pl.BlockSpec((bm, d), lambda i: (i, 0)), pl.BlockSpec((1, d), lambda i: (0, 0))],
        out_specs=pl.BlockSpec((bm, d), lambda i: (i, 0)),
        compiler_params=_params("parallel"),
        name="rmsnorm",
    )(x, w.reshape(1, d))


def _mm_kernel(a_ref, w_ref, o_ref):
    o_ref[...] = _dot(a_ref[...], w_ref[...]).astype(o_ref.dtype)


def _mm_res_kernel(a_ref, w_ref, r_ref, o_ref):
    o_ref[...] = (r_ref[...] + _dot(a_ref[...], w_ref[...])).astype(o_ref.dtype)


def _matmul(a, w, residual=None, out_dtype=jnp.float32, name="matmul"):
    m, k = a.shape
    n = w.shape[1]
    bm, bn = min(MM_BM, m), min(MM_BN, n)
    in_specs = [pl.BlockSpec((bm, k), lambda i, j: (i, 0)), pl.BlockSpec((k, bn), lambda i, j: (0, j))]
    args = [a, w]
    kern = _mm_kernel
    if residual is not None:
        in_specs.append(pl.BlockSpec((bm, bn), lambda i, j: (i, j)))
        args.append(residual)
        kern = _mm_res_kernel
    return pl.pallas_call(
        kern,
        out_shape=jax.ShapeDtypeStruct((m, n), out_dtype),
        grid=(m // bm, n // bn),
        in_specs=in_specs,
        out_specs=pl.BlockSpec((bm, bn), lambda i, j: (i, j)),
        compiler_params=_params("parallel", "parallel"),
        name=name,
    )(*args)


def _mm_headnorm_kernel(a_ref, w_ref, g_ref, o_ref, *, scale, norm_every):
    y = _dot(a_ref[...], w_ref[...])
    bn = y.shape[1]

    def normed():
        for c in range(bn // LANE):
            ys = y[:, c * LANE:(c + 1) * LANE]
            inv = lax.rsqrt(jnp.mean(ys * ys, axis=-1, keepdims=True) + EPS)
            o_ref[:, c * LANE:(c + 1) * LANE] = (ys * inv * g_ref[:, c * LANE:(c + 1) * LANE] * scale).astype(o_ref.dtype)

    if norm_every == 1:
        normed()
    else:
        j = pl.program_id(1)

        @pl.when(j % norm_every == 0)
        def _():
            normed()

        @pl.when(j % norm_every != 0)
        def _():
            o_ref[...] = y.astype(o_ref.dtype)


def _matmul_headnorm(a, w, gain_row, *, scale, bn, norm_every, name):
    m, k = a.shape
    n = w.shape[1]
    bm = min(MM_BM, m)
    return pl.pallas_call(
        functools.partial(_mm_headnorm_kernel, scale=scale, norm_every=norm_every),
        out_shape=jax.ShapeDtypeStruct((m, n), jnp.float32),
        grid=(m // bm, n // bn),
        in_specs=[
            pl.BlockSpec((bm, k), lambda i, j: (i, 0)),
            pl.BlockSpec((k, bn), lambda i, j: (0, j)),
            pl.BlockSpec((1, bn), lambda i, j: (0, j)),
        ],
        out_specs=pl.BlockSpec((bm, bn), lambda i, j: (i, j)),
        compiler_params=_params("parallel", "parallel"),
        name=name,
    )(a, w, gain_row)


def _swiglu_up_kernel(h_ref, wg_ref, wu_ref, o_ref, *, hidden):
    h = h_ref[...]
    g = _dot(h, wg_ref[...])
    u = _dot(h, wu_ref[...])
    act = g * jax.nn.sigmoid(g) * u
    bn = act.shape[1]
    col = pl.program_id(1) * bn + lax.broadcasted_iota(jnp.int32, act.shape, 1)
    o_ref[...] = jnp.where(col < hidden, act, 0.0).astype(o_ref.dtype)


def _swiglu_up(h, wg, wu, hidden_padded):
    m, k = h.shape
    hidden = wg.shape[1]
    bm, bn = min(MM_BM, m), SWIGLU_BN
    return pl.pallas_call(
        functools.partial(_swiglu_up_kernel, hidden=hidden),
        out_shape=jax.ShapeDtypeStruct((m, hidden_padded), jnp.bfloat16),
        grid=(m // bm, hidden_padded // bn),
        in_specs=[
            pl.BlockSpec((bm, k), lambda i, j: (i, 0)),
            pl.BlockSpec((k, bn), lambda i, j: (0, j)),
            pl.BlockSpec((k, bn), lambda i, j: (0, j)),
        ],
        out_specs=pl.BlockSpec((bm, bn), lambda i, j: (i, j)),
        compiler_params=_params("parallel", "parallel"),
        name="swiglu_up",
    )(h, wg, wu)


def _mm_acc_res_kernel(a_ref, w_ref, r_ref, o_ref, acc_ref):
    kk = pl.program_id(2)

    @pl.when(kk == 0)
    def _():
        acc_ref[...] = r_ref[...]

    acc_ref[...] += _dot(a_ref[...], w_ref[...])

    @pl.when(kk == pl.num_programs(2) - 1)
    def _():
        o_ref[...] = acc_ref[...]


def _matmul_ktiled_res(a, w, residual, bk, name):
    m, k = a.shape
    n = w.shape[1]
    bm, bn = min(MM_BM, m), min(MM_BN, n)
    return pl.pallas_call(
        _mm_acc_res_kernel,
        out_shape=jax.ShapeDtypeStruct((m, n), jnp.float32),
        grid=(m // bm, n // bn, k // bk),
        in_specs=[
            pl.BlockSpec((bm, bk), lambda i, j, kk: (i, kk)),
            pl.BlockSpec((bk, bn), lambda i, j, kk: (kk, j)),
            pl.BlockSpec((bm, bn), lambda i, j, kk: (i, j)),
        ],
        out_specs=pl.BlockSpec((bm, bn), lambda i, j, kk: (i, j)),
        scratch_shapes=[pltpu.VMEM((bm, bn), jnp.float32)],
        compiler_params=_params("parallel", "parallel", "arbitrary"),
        name=name,
    )(a, w, residual)


def _gdn_gates_kernel(h_ref, w_ref, alog_ref, dt_ref, o_ref):
    nh = alog_ref.shape[0]
    yt = _dot(h_ref[...], w_ref[...]).T
    o_ref[0] = jax.nn.sigmoid(yt[:nh])
    x = yt[nh:2 * nh] + dt_ref[...]
    softplus = jnp.maximum(x, 0.0) + jnp.log1p(jnp.exp(-jnp.abs(x)))
    g = -jnp.exp(alog_ref[...]) * softplus
    lane = lax.broadcasted_iota(jnp.int32, (nh, GDN_CHUNK), 1)
    for c in range(g.shape[1] // GDN_CHUNK):
        acc = g[:, c * GDN_CHUNK:(c + 1) * GDN_CHUNK]
        shift = 1
        while shift < GDN_CHUNK:
            acc = acc + jnp.where(lane >= shift, pltpu.roll(acc, shift, 1), 0.0)
            shift *= 2
        o_ref[1, :, c * GDN_CHUNK:(c + 1) * GDN_CHUNK] = acc


def _gdn_gates(h, w_gate, a_log, dt_bias):
    s, k = h.shape
    nh = a_log.shape[0]
    bm = min(GATES_BM, s)
    w_pad = jnp.pad(_bf(w_gate), ((0, 0), (0, LANE - 2 * nh)))
    out = pl.pallas_call(
        _gdn_gates_kernel,
        out_shape=jax.ShapeDtypeStruct((2, nh, s), jnp.float32),
        grid=(s // bm,),
        in_specs=[
            pl.BlockSpec((bm, k), lambda i: (i, 0)),
            pl.BlockSpec((k, LANE), lambda i: (0, 0)),
            pl.BlockSpec((nh, 1), lambda i: (0, 0)),
            pl.BlockSpec((nh, 1), lambda i: (0, 0)),
        ],
        out_specs=pl.BlockSpec((2, nh, bm), lambda i: (0, 0, i)),
        compiler_params=_params("parallel"),
        name="gdn_gates",
    )(h, w_pad, a_log.reshape(nh, 1), dt_bias.reshape(nh, 1))
    return out.reshape(2, nh, 1, s)


def _gdn_core_kernel(q_ref, k_ref, v_ref, z_ref, cwq_ref, cwk_ref, cwv_ref, gate_ref, onorm_ref, o_ref,
                     state_ref, extq_ref, extk_ref, extv_ref, kq_ref, bo_ref):
    tb = q_ref.shape[0]
    hp = q_ref.shape[1] // HEAD_DIM
    c = GDN_CHUNK
    nt = tb // c

    @pl.when(pl.program_id(1) == 0)
    def _():
        state_ref[...] = jnp.zeros_like(state_ref)
        for ext in (extq_ref, extk_ref, extv_ref):
            ext[0:SUBLANE, :] = jnp.zeros((SUBLANE, hp * HEAD_DIM), jnp.float32)

    def conv_silu(x_ref, ext, cw_ref):
        ext[SUBLANE:SUBLANE + tb, :] = x_ref[...]
        acc = ext[pl.ds(SUBLANE, tb), :] * cw_ref[GDN_CONV - 1:GDN_CONV, :]
        for back in range(1, GDN_CONV):
            acc = acc + ext[pl.ds(SUBLANE - back, tb), :] * cw_ref[GDN_CONV - 1 - back:GDN_CONV - back, :]
        ext[0:SUBLANE, :] = ext[tb:tb + SUBLANE, :]
        return acc * jax.nn.sigmoid(acc)

    def l2norm(x):
        return x * lax.rsqrt(jnp.sum(x * x, axis=-1, keepdims=True) + EPS)

    def head_cols(x, hh):
        return x[:, hh * HEAD_DIM:(hh + 1) * HEAD_DIM]

    q_conv = conv_silu(q_ref, extq_ref, cwq_ref)
    k_conv = conv_silu(k_ref, extk_ref, cwk_ref)
    v_conv = conv_silu(v_ref, extv_ref, cwv_ref)
    q_all = [l2norm(head_cols(q_conv, hh)) * (HEAD_DIM ** -0.5) for hh in range(hp)]
    k_all = [l2norm(head_cols(k_conv, hh)) for hh in range(hp)]
    v_all = [head_cols(v_conv, hh) for hh in range(hp)]

    row = lax.broadcasted_iota(jnp.int32, (c, c), 0)
    col = lax.broadcasted_iota(jnp.int32, (c, c), 1)
    causal = row >= col
    strict = row > col
    eye = jnp.where(row == col, 1.0, 0.0)
    pair_masks = [((row >> (l + 1)) == (col >> (l + 1))) & ((row >> l) != (col >> l)) for l in range(c.bit_length() - 1)]

    tiles = [(hh, n) for n in range(nt) for hh in range(hp)]
    q_t, k_t, v_t, kb_t, decay_t, egc_t, beta_t, kdec_t, a_t = {}, {}, {}, {}, {}, {}, {}, {}, {}
    for t in tiles:
        hh, n = t
        sl = slice(n * c, (n + 1) * c)
        q_t[t], k_t[t], v_t[t] = q_all[hh][sl], k_all[hh][sl], v_all[hh][sl]
        gc_row = gate_ref[1, hh, :, sl]
        gc_r = jnp.broadcast_to(gc_row, (c, c))
        gc_c = gc_r.T
        beta_t[t] = jnp.broadcast_to(gate_ref[0, hh, :, sl], (c, c)).T
        decay_t[t] = jnp.where(causal, jnp.exp(jnp.where(causal, gc_c - gc_r, 0.0)), 0.0)
        egc_t[t] = jnp.exp(gc_c)
        kdec_t[t] = k_t[t] * jnp.exp(gc_row[:, c - 1:c] - gc_c)
        kb_t[t] = k_t[t] * beta_t[t]
        a_t[t] = jnp.where(strict, _dot_nt(_bf(kb_t[t]), _bf(k_t[t])) * decay_t[t], 0.0)
    t_t = {t: eye - jnp.where(pair_masks[0], a_t[t], 0.0) for t in tiles}
    for off_mask in pair_masks[1:]:
        t16 = {t: _bf(t_t[t]) for t in tiles}
        ta = {t: _dot(t16[t], _bf(jnp.where(off_mask, a_t[t], 0.0))) for t in tiles}
        t_t = {t: t_t[t] - _dot(_bf(ta[t]), t16[t]) for t in tiles}
    for t in tiles:
        hh, n = t
        rhs = jnp.concatenate([kb_t[t] * egc_t[t], v_t[t] * beta_t[t]], axis=1)
        wu16 = _bf(_dot(_bf(t_t[t]), _bf(rhs)))
        attn = jnp.where(causal, _dot_nt(_bf(q_t[t]), _bf(k_t[t])) * decay_t[t], 0.0)
        kw_ku = _dot(_bf(kdec_t[t].T), wu16)
        aw_au = _dot(_bf(attn), wu16)
        kq_ref[hh, n, 0:c, :] = _bf(kw_ku[:, :HEAD_DIM])
        kq_ref[hh, n, c:2 * c, :] = _bf(q_t[t] * egc_t[t] - aw_au[:, :HEAD_DIM])
        bo_ref[hh, n, 0:c, :] = kw_ku[:, HEAD_DIM:]
        bo_ref[hh, n, c:2 * c, :] = aw_au[:, HEAD_DIM:]

    state = [state_ref[hh] for hh in range(hp)]
    for hh, n in tiles:
        sl = slice(n * c, (n + 1) * c)
        cols = slice(hh * HEAD_DIM, (hh + 1) * HEAD_DIM)
        ss = _dot(kq_ref[hh, n], _bf(state[hh]))
        o = ss[c:] + bo_ref[hh, n, c:2 * c, :]
        g_last = jnp.exp(gate_ref[1, hh, :, (n + 1) * c - 1:(n + 1) * c])
        state[hh] = state[hh] * g_last - ss[:c] + bo_ref[hh, n, 0:c, :]
        o = o * lax.rsqrt(jnp.mean(o * o, axis=-1, keepdims=True) + EPS) * onorm_ref[...]
        z = z_ref[sl, cols]
        o_ref[sl, cols] = (o * (z * jax.nn.sigmoid(z))).astype(o_ref.dtype)
    for hh in range(hp):
        state_ref[hh] = state[hh]


def _gdn_core(proj, conv_w, gates, out_norm, heads):
    s = proj.shape[0]
    tb = min(GDN_TB, s)
    d = HEAD_DIM
    hp = GDN_HEADS_PER_STEP
    assert heads % hp == 0
    n_hb = heads // hp
    col = lambda off: (lambda h, t: (t, off * n_hb + h))
    cw = lambda off: (lambda h, t: (0, off * n_hb + h))
    nt = tb // GDN_CHUNK
    return pl.pallas_call(
        _gdn_core_kernel,
        out_shape=jax.ShapeDtypeStruct((s, heads * d), jnp.bfloat16),
        grid=(n_hb, s // tb),
        in_specs=[
            pl.BlockSpec((tb, hp * d), col(0)), pl.BlockSpec((tb, hp * d), col(1)),
            pl.BlockSpec((tb, hp * d), col(2)), pl.BlockSpec((tb, hp * d), col(3)),
            pl.BlockSpec((GDN_CONV, hp * d), cw(0)), pl.BlockSpec((GDN_CONV, hp * d), cw(1)),
            pl.BlockSpec((GDN_CONV, hp * d), cw(2)),
            pl.BlockSpec((2, hp, 1, tb), lambda h, t: (0, h, 0, t)),
            pl.BlockSpec((1, d), lambda h, t: (0, 0)),
        ],
        out_specs=pl.BlockSpec((tb, hp * d), lambda h, t: (t, h)),
        scratch_shapes=[pltpu.VMEM((hp, d, d), jnp.float32)] + [pltpu.VMEM((tb + SUBLANE, hp * d), jnp.float32)] * 3
        + [pltpu.VMEM((hp, nt, 2 * GDN_CHUNK, d), jnp.bfloat16), pltpu.VMEM((hp, nt, 2 * GDN_CHUNK, d), jnp.float32)],
        compiler_params=_params("parallel", "arbitrary"),
        name="gdn_core",
    )(proj, proj, proj, proj, conv_w, conv_w, conv_w, gates, out_norm.reshape(1, d))


def _dil_attn_kernel(*refs, groups, sb):
    ng = len(groups)
    q_refs = refs[:ng]
    kv_refs = refs[ng:5 * ng]
    o_ref = refs[5 * ng]
    num_refs = refs[5 * ng + 1:6 * ng + 1]
    m_refs = refs[6 * ng + 1:7 * ng + 1]
    den_refs = refs[7 * ng + 1:8 * ng + 1]
    blk = DIL_BLOCK
    not_first = pl.program_id(0) > 0
    row = lax.broadcasted_iota(jnp.int32, (blk, blk), 0)
    col = lax.broadcasted_iota(jnp.int32, (blk, blk), 1)
    neg_inf = jnp.float32(-jnp.inf)

    for gi, (window, dil) in enumerate(groups):
        span = window // dil
        own_mask = (row - col >= 0) & (row - col <= span)
        prev_mask = (row + blk - col) <= span
        prev_mask_first = prev_mask & not_first
        k_own, k_prev, v_own, v_prev = kv_refs[4 * gi:4 * gi + 4]
        n_sub = sb // (blk * dil)
        for r in range(dil):
            for b in range(n_sub):
                rows = lambda start: pl.ds(start, blk, stride=dil) if dil > 1 else pl.ds(start, blk)
                cur = rows(b * blk * dil + r)
                q16 = _bf(q_refs[gi][cur, :])
                if b > 0:
                    prv = rows((b - 1) * blk * dil + r)
                    kp, vp, pmask = k_own[prv, :], v_own[prv, :], prev_mask
                else:
                    prv = rows((n_sub - 1) * blk * dil + r)
                    kp, vp, pmask = k_prev[prv, :], v_prev[prv, :], prev_mask_first
                s_own = jnp.where(own_mask, _dot_nt(q16, _bf(k_own[cur, :])), neg_inf)
                s_prev = jnp.where(pmask, _dot_nt(q16, _bf(kp)), neg_inf)
                m = jnp.maximum(jnp.max(s_own, axis=-1, keepdims=True), jnp.max(s_prev, axis=-1, keepdims=True))
                e_own = jnp.exp(s_own - m)
                e_prev = jnp.exp(s_prev - m)
                den = jnp.sum(e_own, axis=-1, keepdims=True) + jnp.sum(e_prev, axis=-1, keepdims=True)
                num = _dot(_bf(e_own), _bf(v_own[cur, :])) + _dot(_bf(e_prev), _bf(vp))
                num_refs[gi][cur, :] = num
                m_refs[gi][cur, :] = jnp.broadcast_to(m, (blk, LANE))
                den_refs[gi][cur, :] = jnp.broadcast_to(den, (blk, LANE))

    m_all = m_refs[0][...]
    for gi in range(1, ng):
        m_all = jnp.maximum(m_all, m_refs[gi][...])
    num = jnp.zeros((sb, LANE), jnp.float32)
    den = jnp.zeros((sb, LANE), jnp.float32)
    for gi in range(ng):
        wgt = jnp.exp(m_refs[gi][...] - m_all)
        num = num + wgt * num_refs[gi][...]
        den = den + wgt * den_refs[gi][...]
    o_ref[...] = (num / den).astype(o_ref.dtype)


def _dilated_attention(q, kv, heads, kv_heads):
    s = q.shape[0]
    ng = len(DIL_GROUPS)
    sb = ATT_SB
    assert s % sb == 0 and all(w // d <= DIL_BLOCK for w, d in DIL_GROUPS)
    rep = heads // kv_heads
    d = HEAD_DIM
    blk = lambda f: pl.BlockSpec((sb, d), f)
    in_specs = [blk(lambda n, g, r, gi=gi: (n, gi * heads + g * rep + r)) for gi in range(ng)]
    args = [q] * ng
    for gi in range(ng):
        for part in range(2):
            cb = (gi * 2 + part) * kv_heads
            in_specs.append(blk(lambda n, g, r, cb=cb: (n, cb + g)))
            in_specs.append(blk(lambda n, g, r, cb=cb: (jnp.maximum(n - 1, 0), cb + g)))
            args += [kv, kv]
    return pl.pallas_call(
        functools.partial(_dil_attn_kernel, groups=DIL_GROUPS, sb=sb),
        out_shape=jax.ShapeDtypeStruct((s, heads * d), jnp.bfloat16),
        grid=(s // sb, kv_heads, rep),
        in_specs=in_specs,
        out_specs=blk(lambda n, g, r: (n, g * rep + r)),
        scratch_shapes=[pltpu.VMEM((sb, d), jnp.float32)] * (3 * ng),
        compiler_params=_params("parallel", "parallel", "arbitrary"),
        name="dilated_attention",
    )(*args)


def _gdn_layer(x, attn_norm, w_in, conv_w, a_log, dt_bias, out_norm, w_out):
    heads = a_log.shape[0]
    qk = heads * HEAD_DIM
    h = _rmsnorm(x, attn_norm)
    proj = _matmul(h, _bf(w_in[:, :4 * qk]), name="gdn_in_proj")
    gates = _gdn_gates(h, w_in[:, 4 * qk:], a_log, dt_bias)
    o = _gdn_core(proj, conv_w, gates, out_norm, heads)
    return _matmul(o, _bf(w_out), residual=x, name="gdn_out_proj")


def _shared_kv(x, kv_norm, w_kv, k_norm):
    n_dil = k_norm.shape[0]
    kvw = w_kv.shape[1] // (2 * n_dil)
    h = _rmsnorm(x, kv_norm)
    gain = jnp.concatenate([jnp.tile(k_norm, (1, kvw // HEAD_DIM)), jnp.ones((n_dil, kvw), jnp.float32)], axis=1).reshape(1, -1)
    return _matmul_headnorm(h, _bf(w_kv), gain, scale=1.0, bn=kvw, norm_every=2, name="kv_proj")


def _dilated_layer(x, kv, attn_norm, w_q, q_norm, w_out):
    n_dil = q_norm.shape[0]
    qw = w_q.shape[1] // n_dil
    kvw = kv.shape[1] // (2 * n_dil)
    heads, kv_heads = qw // HEAD_DIM, kvw // HEAD_DIM
    h = _rmsnorm(x, attn_norm)
    gain = jnp.tile(q_norm, (1, heads)).reshape(1, -1)
    q = _matmul_headnorm(h, _bf(w_q), gain, scale=HEAD_DIM ** -0.5, bn=min(MM_BN, qw), norm_every=1, name="q_proj")
    o = _dilated_attention(q, kv, heads, kv_heads)
    return _matmul(o, _bf(w_out), residual=x, name="attn_out_proj")


def _ffn_layer(x, norm_w, w_gate_up, w_down):
    hidden = w_down.shape[0]
    hidden_padded = -(-hidden // MM_BN) * MM_BN
    h = _rmsnorm(x, norm_w)
    act = _swiglu_up(h, _bf(w_gate_up[:, :hidden]), _bf(w_gate_up[:, hidden:]), hidden_padded)
    w_down_p = jnp.pad(_bf(w_down), ((0, hidden_padded - hidden), (0, 0)))
    return _matmul_ktiled_res(act, w_down_p, x, _pick_down_bk(hidden_padded), name="ffn_down")


def kernel(x, a_attn_norm, a_w_in, a_conv_w, a_a_log, a_dt_bias, a_out_norm, a_w_out, kv_norm, w_kv, k_norm,
           b_attn_norm, b_w_q, b_q_norm, b_w_out, ffn_norm, ffn_w_gate_up, ffn_w_down):
    batch, s, d = x.shape
    assert batch == 1
    x = x.reshape(s, d)
    depth = ffn_norm.shape[0]
    n_a = a_attn_norm.shape[0]
    kv = None
    for layer in range(depth):
        if layer < n_a:
            i = layer
            x = _gdn_layer(x, a_attn_norm[i], a_w_in[i], a_conv_w[i], a_a_log[i], a_dt_bias[i], a_out_norm[i], a_w_out[i])
        else:
            if layer == n_a:
                kv = _shared_kv(x, kv_norm, w_kv, k_norm)
            j = layer - n_a
            x = _dilated_layer(x, kv, b_attn_norm[j], b_w_q[j], b_q_norm[j], b_w_out[j])
        x = _ffn_layer(x, ffn_norm[layer], ffn_w_gate_up[layer], ffn_w_down[layer])
    return x.reshape(batch, s, d)
```

```python
import functools

import jax
import jax.numpy as jnp
from jax import lax
from jax.experimental import pallas as pl
from jax.experimental.pallas import tpu as pltpu

V7X_VMEM_LIMIT_BYTES = 56 * 1024 * 1024
LANE = 128
SUBLANE = 8

EPS = 1e-6
GDN_CONV = 4
DIL_GROUPS = ((128, 1), (512, 4), (2048, 16))
DIL_BLOCK = 128
HEAD_DIM = 128

MM_BM = 1024
MM_BN = 1024
SWIGLU_BN = 512
DOWN_BK_MAX = 3072
DOWN_BK_STEP = 256
NORM_BM = 256

GDN_CHUNK = 128
GDN_TB = 1024
GDN_HEADS_PER_STEP = 2
GATES_BM = 512
ATT_RES = max(d for _, d in DIL_GROUPS)
ATT_SB = DIL_BLOCK * ATT_RES
ATT_HEADS_PER_STEP = 2
ATT_TILE_BATCH = 4
ATT_NORM_BI = 16
ATT_OUT_BN = 512


def _pick_down_bk(k_padded):
    best = DOWN_BK_STEP
    for bk in range(DOWN_BK_STEP, DOWN_BK_MAX + 1, DOWN_BK_STEP):
        if k_padded % bk == 0:
            best = bk
    return best


def _params(*semantics):
    return pltpu.CompilerParams(dimension_semantics=semantics, vmem_limit_bytes=V7X_VMEM_LIMIT_BYTES)


def _dot(a, b):
    return jnp.dot(a, b, preferred_element_type=jnp.float32)


def _dot_nt(a, b):
    return lax.dot_general(a, b, (((1,), (1,)), ((), ())), preferred_element_type=jnp.float32)


def _bf(x):
    return x.astype(jnp.bfloat16)


def _rmsnorm_kernel(x_ref, w_ref, o_ref):
    x = x_ref[...]
    y = x * lax.rsqrt(jnp.mean(x * x, axis=-1, keepdims=True) + EPS)
    o_ref[...] = (y * w_ref[...]).astype(o_ref.dtype)


def _rmsnorm(x, w, out_dtype=jnp.bfloat16):
    s, d = x.shape
    bm = min(NORM_BM, s)
    return pl.pallas_call(
        _rmsnorm_kernel,
        out_shape=jax.ShapeDtypeStruct((s, d), out_dtype),
        grid=(s // bm,),
        in_specs=[pl.BlockSpec((bm, d), lambda i: (i, 0)), pl.BlockSpec((1, d), lambda i: (0, 0))],
        out_specs=pl.BlockSpec((bm, d), lambda i: (i, 0)),
        compiler_params=_params("parallel"),
        name="rmsnorm",
    )(x, w.reshape(1, d))


def _mm_kernel(a_ref, w_ref, o_ref):
    o_ref[...] = _dot(a_ref[...], w_ref[...]).astype(o_ref.dtype)


def _mm_res_kernel(a_ref, w_ref, r_ref, o_ref):
    o_ref[...] = (r_ref[...] + _dot(a_ref[...], w_ref[...])).astype(o_ref.dtype)


def _matmul(a, w, residual=None, out_dtype=jnp.float32, name="matmul"):
    m, k = a.shape
    n = w.shape[1]
    bm, bn = min(MM_BM, m), min(MM_BN, n)
    in_specs = [pl.BlockSpec((bm, k), lambda i, j: (i, 0)), pl.BlockSpec((k, bn), lambda i, j: (0, j))]
    args = [a, w]
    kern = _mm_kernel
    if residual is not None:
        in_specs.append(pl.BlockSpec((bm, bn), lambda i, j: (i, j)))
        args.append(residual)
        kern = _mm_res_kernel
    return pl.pallas_call(
        kern,
        out_shape=jax.ShapeDtypeStruct((m, n), out_dtype),
        grid=(m // bm, n // bn),
        in_specs=in_specs,
        out_specs=pl.BlockSpec((bm, bn), lambda i, j: (i, j)),
        compiler_params=_params("parallel", "parallel"),
        name=name,
    )(*args)


def _mm_headnorm_kernel(a_ref, w_ref, g_ref, o_ref, *, scale, norm_every):
    y = _dot(a_ref[...], w_ref[...])
    bn = y.shape[1]

    def normed():
        for c in range(bn // LANE):
            ys = y[:, c * LANE:(c + 1) * LANE]
            inv = lax.rsqrt(jnp.mean(ys * ys, axis=-1, keepdims=True) + EPS)
            o_ref[:, c * LANE:(c + 1) * LANE] = (ys * inv * g_ref[:, c * LANE:(c + 1) * LANE] * scale).astype(o_ref.dtype)

    if norm_every == 1:
        normed()
    else:
        j = pl.program_id(1)

        @pl.when(j % norm_every == 0)
        def _():
            normed()

        @pl.when(j % norm_every != 0)
        def _():
            o_ref[...] = y.astype(o_ref.dtype)


def _matmul_headnorm(a, w, gain_row, *, scale, bn, norm_every, name):
    m, k = a.shape
    n = w.shape[1]
    bm = min(MM_BM, m)
    return pl.pallas_call(
        functools.partial(_mm_headnorm_kernel, scale=scale, norm_every=norm_every),
        out_shape=jax.ShapeDtypeStruct((m, n), jnp.float32),
        grid=(m // bm, n // bn),
        in_specs=[
            pl.BlockSpec((bm, k), lambda i, j: (i, 0)),
            pl.BlockSpec((k, bn), lambda i, j: (0, j)),
            pl.BlockSpec((1, bn), lambda i, j: (0, j)),
        ],
        out_specs=pl.BlockSpec((bm, bn), lambda i, j: (i, j)),
        compiler_params=_params("parallel", "parallel"),
        name=name,
    )(a, w, gain_row)


def _swiglu_up_kernel(h_ref, wg_ref, wu_ref, o_ref, *, hidden):
    h = h_ref[...]
    g = _dot(h, wg_ref[...])
    u = _dot(h, wu_ref[...])
    act = g * jax.nn.sigmoid(g) * u
    bn = act.shape[1]
    col = pl.program_id(1) * bn + lax.broadcasted_iota(jnp.int32, act.shape, 1)
    o_ref[...] = jnp.where(col < hidden, act, 0.0).astype(o_ref.dtype)


def _swiglu_up(h, wg, wu, hidden_padded):
    m, k = h.shape
    hidden = wg.shape[1]
    bm, bn = min(MM_BM, m), SWIGLU_BN
    return pl.pallas_call(
        functools.partial(_swiglu_up_kernel, hidden=hidden),
        out_shape=jax.ShapeDtypeStruct((m, hidden_padded), jnp.bfloat16),
        grid=(m // bm, hidden_padded // bn),
        in_specs=[
            pl.BlockSpec((bm, k), lambda i, j: (i, 0)),
            pl.BlockSpec((k, bn), lambda i, j: (0, j)),
            pl.BlockSpec((k, bn), lambda i, j: (0, j)),
        ],
        out_specs=pl.BlockSpec((bm, bn), lambda i, j: (i, j)),
        compiler_params=_params("parallel", "parallel"),
        name="swiglu_up",
    )(h, wg, wu)


def _mm_acc_res_kernel(a_ref, w_ref, r_ref, o_ref, acc_ref):
    kk = pl.program_id(2)

    @pl.when(kk == 0)
    def _():
        acc_ref[...] = r_ref[...]

    acc_ref[...] += _dot(a_ref[...], w_ref[...])

    @pl.when(kk == pl.num_programs(2) - 1)
    def _():
        o_ref[...] = acc_ref[...]


def _matmul_ktiled_res(a, w, residual, bk, name):
    m, k = a.shape
    n = w.shape[1]
    bm, bn = min(MM_BM, m), min(MM_BN, n)
    return pl.pallas_call(
        _mm_acc_res_kernel,
        out_shape=jax.ShapeDtypeStruct((m, n), jnp.float32),
        grid=(m // bm, n // bn, k // bk),
        in_specs=[
            pl.BlockSpec((bm, bk), lambda i, j, kk: (i, kk)),
            pl.BlockSpec((bk, bn), lambda i, j, kk: (kk, j)),
            pl.BlockSpec((bm, bn), lambda i, j, kk: (i, j)),
        ],
        out_specs=pl.BlockSpec((bm, bn), lambda i, j, kk: (i, j)),
        scratch_shapes=[pltpu.VMEM((bm, bn), jnp.float32)],
        compiler_params=_params("parallel", "parallel", "arbitrary"),
        name=name,
    )(a, w, residual)


def _gdn_gates_kernel(h_ref, w_ref, alog_ref, dt_ref, o_ref):
    nh = alog_ref.shape[0]
    yt = _dot(h_ref[...], w_ref[...]).T
    o_ref[0] = jax.nn.sigmoid(yt[:nh])
    x = yt[nh:2 * nh] + dt_ref[...]
    softplus = jnp.maximum(x, 0.0) + jnp.log1p(jnp.exp(-jnp.abs(x)))
    g = -jnp.exp(alog_ref[...]) * softplus
    lane = lax.broadcasted_iota(jnp.int32, (nh, GDN_CHUNK), 1)
    for c in range(g.shape[1] // GDN_CHUNK):
        acc = g[:, c * GDN_CHUNK:(c + 1) * GDN_CHUNK]
        shift = 1
        while shift < GDN_CHUNK:
            acc = acc + jnp.where(lane >= shift, pltpu.roll(acc, shift, 1), 0.0)
            shift *= 2
        o_ref[1, :, c * GDN_CHUNK:(c + 1) * GDN_CHUNK] = acc


def _gdn_gates(h, w_gate, a_log, dt_bias):
    s, k = h.shape
    nh = a_log.shape[0]
    bm = min(GATES_BM, s)
    w_pad = jnp.pad(_bf(w_gate), ((0, 0), (0, LANE - 2 * nh)))
    out = pl.pallas_call(
        _gdn_gates_kernel,
        out_shape=jax.ShapeDtypeStruct((2, nh, s), jnp.float32),
        grid=(s // bm,),
        in_specs=[
            pl.BlockSpec((bm, k), lambda i: (i, 0)),
            pl.BlockSpec((k, LANE), lambda i: (0, 0)),
            pl.BlockSpec((nh, 1), lambda i: (0, 0)),
            pl.BlockSpec((nh, 1), lambda i: (0, 0)),
        ],
        out_specs=pl.BlockSpec((2, nh, bm), lambda i: (0, 0, i)),
        compiler_params=_params("parallel"),
        name="gdn_gates",
    )(h, w_pad, a_log.reshape(nh, 1), dt_bias.reshape(nh, 1))
    return out.reshape(2, nh, 1, s)


def _gdn_core_kernel(q_ref, k_ref, v_ref, z_ref, cwq_ref, cwk_ref, cwv_ref, gate_ref, onorm_ref, o_ref,
                     state_ref, extq_ref, extk_ref, extv_ref, kq_ref, bo_ref):
    tb = q_ref.shape[0]
    hp = q_ref.shape[1] // HEAD_DIM
    c = GDN_CHUNK
    nt = tb // c

    @pl.when(pl.program_id(1) == 0)
    def _():
        state_ref[...] = jnp.zeros_like(state_ref)
        for ext in (extq_ref, extk_ref, extv_ref):
            ext[0:SUBLANE, :] = jnp.zeros((SUBLANE, hp * HEAD_DIM), jnp.float32)

    def conv_silu(x_ref, ext, cw_ref):
        ext[SUBLANE:SUBLANE + tb, :] = x_ref[...]
        acc = ext[pl.ds(SUBLANE, tb), :] * cw_ref[GDN_CONV - 1:GDN_CONV, :]
        for back in range(1, GDN_CONV):
            acc = acc + ext[pl.ds(SUBLANE - back, tb), :] * cw_ref[GDN_CONV - 1 - back:GDN_CONV - back, :]
        ext[0:SUBLANE, :] = ext[tb:tb + SUBLANE, :]
        return acc * jax.nn.sigmoid(acc)

    def l2norm(x):
        return x * lax.rsqrt(jnp.sum(x * x, axis=-1, keepdims=True) + EPS)

    def head_cols(x, hh):
        return x[:, hh * HEAD_DIM:(hh + 1) * HEAD_DIM]

    q_conv = conv_silu(q_ref, extq_ref, cwq_ref)
    k_conv = conv_silu(k_ref, extk_ref, cwk_ref)
    v_conv = conv_silu(v_ref, extv_ref, cwv_ref)
    q_all = [l2norm(head_cols(q_conv, hh)) * (HEAD_DIM ** -0.5) for hh in range(hp)]
    k_all = [l2norm(head_cols(k_conv, hh)) for hh in range(hp)]
    v_all = [head_cols(v_conv, hh) for hh in range(hp)]

    row = lax.broadcasted_iota(jnp.int32, (c, c), 0)
    col = lax.broadcasted_iota(jnp.int32, (c, c), 1)
    causal = row >= col
    strict = row > col
    eye = jnp.where(row == col, 1.0, 0.0)
    pair_masks = [((row >> (l + 1)) == (col >> (l + 1))) & ((row >> l) != (col >> l)) for l in range(c.bit_length() - 1)]

    tiles = [(hh, n) for n in range(nt) for hh in range(hp)]
    q_t, k_t, v_t, kb_t, decay_t, egc_t, beta_t, kdec_t, a_t = {}, {}, {}, {}, {}, {}, {}, {}, {}
    for t in tiles:
        hh, n = t
        sl = slice(n * c, (n + 1) * c)
        q_t[t], k_t[t], v_t[t] = q_all[hh][sl], k_all[hh][sl], v_all[hh][sl]
        gc_row = gate_ref[1, hh, :, sl]
        gc_r = jnp.broadcast_to(gc_row, (c, c))
        gc_c = gc_r.T
        beta_t[t] = jnp.broadcast_to(gate_ref[0, hh, :, sl], (c, c)).T
        decay_t[t] = jnp.where(causal, jnp.exp(jnp.where(causal, gc_c - gc_r, 0.0)), 0.0)
        egc_t[t] = jnp.exp(gc_c)
        kdec_t[t] = k_t[t] * jnp.exp(gc_row[:, c - 1:c] - gc_c)
        kb_t[t] = k_t[t] * beta_t[t]
        a_t[t] = jnp.where(strict, _dot_nt(_bf(kb_t[t]), _bf(k_t[t])) * decay_t[t], 0.0)
    t_t = {t: eye - jnp.where(pair_masks[0], a_t[t], 0.0) for t in tiles}
    a16 = {t: _bf(a_t[t]) for t in tiles}
    for off_mask in pair_masks[1:]:
        t16 = {t: _bf(t_t[t]) for t in tiles}
        ta = {t: _dot(t16[t], a16[t]) for t in tiles}
        t_t = {t: jnp.where(off_mask, t_t[t] - _dot(_bf(ta[t]), t16[t]), t_t[t]) for t in tiles}
    for t in tiles:
        hh, n = t
        rhs = jnp.concatenate([kb_t[t] * egc_t[t], v_t[t] * beta_t[t]], axis=1)
        wu16 = _bf(_dot(_bf(t_t[t]), _bf(rhs)))
        attn = jnp.where(causal, _dot_nt(_bf(q_t[t]), _bf(k_t[t])) * decay_t[t], 0.0)
        kw_ku = _dot(_bf(kdec_t[t].T), wu16)
        aw_au = _dot(_bf(attn), wu16)
        kq_ref[hh, n, 0:c, :] = _bf(kw_ku[:, :HEAD_DIM])
        kq_ref[hh, n, c:2 * c, :] = _bf(q_t[t] * egc_t[t] - aw_au[:, :HEAD_DIM])
        bo_ref[hh, n, 0:c, :] = kw_ku[:, HEAD_DIM:]
        bo_ref[hh, n, c:2 * c, :] = aw_au[:, HEAD_DIM:]

    state = [state_ref[hh] for hh in range(hp)]
    for hh, n in tiles:
        sl = slice(n * c, (n + 1) * c)
        cols = slice(hh * HEAD_DIM, (hh + 1) * HEAD_DIM)
        ss = _dot(kq_ref[hh, n], _bf(state[hh]))
        o = ss[c:] + bo_ref[hh, n, c:2 * c, :]
        g_last = jnp.exp(gate_ref[1, hh, :, (n + 1) * c - 1:(n + 1) * c])
        state[hh] = state[hh] * g_last - ss[:c] + bo_ref[hh, n, 0:c, :]
        o = o * lax.rsqrt(jnp.mean(o * o, axis=-1, keepdims=True) + EPS) * onorm_ref[...]
        z = z_ref[sl, cols]
        o_ref[sl, cols] = (o * (z * jax.nn.sigmoid(z))).astype(o_ref.dtype)
    for hh in range(hp):
        state_ref[hh] = state[hh]


def _gdn_core(proj, conv_w, gates, out_norm, heads):
    s = proj.shape[0]
    tb = min(GDN_TB, s)
    d = HEAD_DIM
    hp = GDN_HEADS_PER_STEP
    assert heads % hp == 0
    n_hb = heads // hp
    col = lambda off: (lambda h, t: (t, off * n_hb + h))
    cw = lambda off: (lambda h, t: (0, off * n_hb + h))
    nt = tb // GDN_CHUNK
    return pl.pallas_call(
        _gdn_core_kernel,
        out_shape=jax.ShapeDtypeStruct((s, heads * d), jnp.bfloat16),
        grid=(n_hb, s // tb),
        in_specs=[
            pl.BlockSpec((tb, hp * d), col(0)), pl.BlockSpec((tb, hp * d), col(1)),
            pl.BlockSpec((tb, hp * d), col(2)), pl.BlockSpec((tb, hp * d), col(3)),
            pl.BlockSpec((GDN_CONV, hp * d), cw(0)), pl.BlockSpec((GDN_CONV, hp * d), cw(1)),
            pl.BlockSpec((GDN_CONV, hp * d), cw(2)),
            pl.BlockSpec((2, hp, 1, tb), lambda h, t: (0, h, 0, t)),
            pl.BlockSpec((1, d), lambda h, t: (0, 0)),
        ],
        out_specs=pl.BlockSpec((tb, hp * d), lambda h, t: (t, h)),
        scratch_shapes=[pltpu.VMEM((hp, d, d), jnp.float32)] + [pltpu.VMEM((tb + SUBLANE, hp * d), jnp.float32)] * 3
        + [pltpu.VMEM((hp, nt, 2 * GDN_CHUNK, d), jnp.bfloat16), pltpu.VMEM((hp, nt, 2 * GDN_CHUNK, d), jnp.float32)],
        compiler_params=_params("parallel", "arbitrary"),
        name="gdn_core",
    )(proj, proj, proj, proj, conv_w, conv_w, conv_w, gates, out_norm.reshape(1, d))


def _rmsnorm_rm_kernel(x_ref, w_ref, o_ref):
    x = x_ref[...]
    y = _bf(x * lax.rsqrt(jnp.mean(x * x, axis=-1, keepdims=True) + EPS) * w_ref[...])
    n = x.shape[0]
    bi = n // ATT_RES
    row = lax.broadcasted_iota(jnp.int32, (n, n), 0)
    col = lax.broadcasted_iota(jnp.int32, (n, n), 1)
    perm = _bf(jnp.where(col == ATT_RES * (row % bi) + row // bi, 1.0, 0.0))
    out = _dot(perm, y)
    for r in range(ATT_RES):
        o_ref[r] = out[r * bi:(r + 1) * bi].astype(o_ref.dtype)


def _rmsnorm_residue_major(x, w):
    s, d = x.shape
    n_sb = s // ATT_SB
    bi = ATT_NORM_BI
    steps = DIL_BLOCK // bi
    out = pl.pallas_call(
        _rmsnorm_rm_kernel,
        out_shape=jax.ShapeDtypeStruct((n_sb, ATT_RES, DIL_BLOCK, d), jnp.bfloat16),
        grid=(n_sb, steps),
        in_specs=[pl.BlockSpec((bi * ATT_RES, d), lambda n, i: (n * steps + i, 0)),
                  pl.BlockSpec((1, d), lambda n, i: (0, 0))],
        out_specs=pl.BlockSpec((None, ATT_RES, bi, d), lambda n, i: (n, 0, i, 0)),
        compiler_params=_params("parallel", "parallel"),
        name="rmsnorm_residue_major",
    )(x, w.reshape(1, d))
    return out.reshape(s, d)


def _mm_res_natural_kernel(a_ref, w_ref, r_ref, o_ref):
    acc = _dot(a_ref[...], w_ref[...])
    for r in range(ATT_RES):
        o_ref[:, r, :] = r_ref[:, r, :] + acc[r * DIL_BLOCK:(r + 1) * DIL_BLOCK, :]


def _matmul_res_from_residue_major(a, w, residual, name):
    m, k = a.shape
    n = w.shape[1]
    bn = min(ATT_OUT_BN, n)
    nat = pl.BlockSpec((DIL_BLOCK, ATT_RES, bn), lambda i, j: (i, 0, j))
    out = pl.pallas_call(
        _mm_res_natural_kernel,
        out_shape=jax.ShapeDtypeStruct((m // ATT_RES, ATT_RES, n), jnp.float32),
        grid=(m // ATT_SB, n // bn),
        in_specs=[pl.BlockSpec((ATT_SB, k), lambda i, j: (i, 0)), pl.BlockSpec((k, bn), lambda i, j: (0, j)), nat],
        out_specs=nat,
        compiler_params=_params("parallel", "parallel"),
        name=name,
    )(a, w, residual.reshape(m // ATT_RES, ATT_RES, n))
    return out.reshape(m, n)


def _dil_attn_kernel(*refs, groups, heads_per_step):
    ng = len(groups)
    hs = heads_per_step
    q_refs = refs[:ng]
    kv_refs = refs[ng:5 * ng]
    o_ref = refs[5 * ng]
    og_refs = refs[5 * ng + 1:6 * ng + 1]
    lse_refs = refs[6 * ng + 1:7 * ng + 1]
    blk = DIL_BLOCK
    not_first = pl.program_id(0) > 0
    row = lax.broadcasted_iota(jnp.int32, (hs * blk, 2 * blk), 0) % blk
    col = lax.broadcasted_iota(jnp.int32, (hs * blk, 2 * blk), 1)
    is_prev = col < blk
    col = col % blk
    ones16 = jnp.ones((2 * blk, HEAD_DIM), jnp.bfloat16)
    neg_inf = jnp.float32(-jnp.inf)

    for gi, (window, dil) in enumerate(groups):
        span = window // dil
        per = ATT_RES // dil
        chunk = blk // per
        jq = per * (row % chunk) + row // chunk
        jk = per * (col % chunk) + col // chunk
        own_ok = (~is_prev) & (jq - jk >= 0) & (jq - jk <= span)
        prev_ok = is_prev & (jq + blk - jk <= span)
        mask = own_ok | prev_ok
        mask_first = own_ok | (prev_ok & not_first)
        k_own, k_prev, v_own, v_prev = kv_refs[4 * gi:4 * gi + 4]

        def tile_rows(ref, r_d, lo, cols=slice(None)):
            return jnp.concatenate([ref[r_d + dil * m, lo:lo + chunk, cols] for m in range(per)], axis=0)

        tiles = [(r_d, b) for r_d in range(dil) for b in range(per)]
        for t0 in range(0, len(tiles), ATT_TILE_BATCH):
            batch = tiles[t0:t0 + ATT_TILE_BATCH]
            q16, k16, v16 = {}, {}, {}
            for t in batch:
                r_d, b = t
                lo = chunk * b
                q16[t] = _bf(jnp.concatenate(
                    [tile_rows(q_refs[gi], r_d, lo, slice(h * HEAD_DIM, (h + 1) * HEAD_DIM)) for h in range(hs)], axis=0))
                if b > 0:
                    kp, vp = tile_rows(k_own, r_d, lo - chunk), tile_rows(v_own, r_d, lo - chunk)
                else:
                    kp, vp = tile_rows(k_prev, r_d, 0), tile_rows(v_prev, r_d, 0)
                k16[t] = _bf(jnp.concatenate([kp, tile_rows(k_own, r_d, lo)], axis=0))
                v16[t] = jnp.concatenate([_bf(jnp.concatenate([vp, tile_rows(v_own, r_d, lo)], axis=0)), ones16], axis=1)
            s = {t: jnp.where(mask if t[1] > 0 else mask_first, _dot_nt(q16[t], k16[t]), neg_inf) for t in batch}
            mx = {t: jnp.max(jnp.maximum(s[t][:, :blk], s[t][:, blk:]), axis=-1, keepdims=True) for t in batch}
            e16 = {t: _bf(jnp.exp(s[t] - mx[t])) for t in batch}
            pv = {t: _dot(e16[t], v16[t]) for t in batch}
            for t in batch:
                r_d, b = t
                den = pv[t][:, HEAD_DIM:]
                o_t = pv[t][:, :HEAD_DIM] / den
                lse_t = mx[t] + jnp.log(den)
                for h in range(hs):
                    for m in range(per):
                        src = slice(h * blk + m * chunk, h * blk + (m + 1) * chunk)
                        og_refs[gi][h, r_d + dil * m, chunk * b:chunk * (b + 1), :] = o_t[src]
                        lse_refs[gi][h, r_d + dil * m, chunk * b:chunk * (b + 1), :] = lse_t[src]

    for h in range(hs):
        lse_max = lse_refs[0][h]
        for gi in range(1, ng):
            lse_max = jnp.maximum(lse_max, lse_refs[gi][h])
        num = jnp.zeros(lse_max.shape, jnp.float32)
        den = jnp.zeros(lse_max.shape, jnp.float32)
        for gi in range(ng):
            wgt = jnp.exp(lse_refs[gi][h] - lse_max)
            num = num + wgt * og_refs[gi][h]
            den = den + wgt
        o_ref[:, :, h * HEAD_DIM:(h + 1) * HEAD_DIM] = (num / den).astype(o_ref.dtype)


def _dilated_attention(q, kv, heads, kv_heads):
    s = q.shape[0]
    ng = len(DIL_GROUPS)
    res, blk, d, hs = ATT_RES, DIL_BLOCK, HEAD_DIM, ATT_HEADS_PER_STEP
    assert s % ATT_SB == 0 and all(w // dil <= blk and res % dil == 0 for w, dil in DIL_GROUPS)
    n_sb = s // ATT_SB
    rep = heads // kv_heads
    assert rep % hs == 0
    q4 = q.reshape(n_sb, res, blk, q.shape[1])
    kv4 = kv.reshape(n_sb, res, blk, kv.shape[1])
    in_specs = [pl.BlockSpec((None, res, blk, hs * d),
                             lambda n, g, r, gi=gi: (n, 0, 0, (gi * heads + g * rep) // hs + r)) for gi in range(ng)]
    args = [q4] * ng
    for gi, (_, dil) in enumerate(DIL_GROUPS):
        chunk = blk // (res // dil)
        for part in range(2):
            cb = (gi * 2 + part) * kv_heads
            in_specs.append(pl.BlockSpec((None, res, blk, d), lambda n, g, r, cb=cb: (n, 0, 0, cb + g)))
            in_specs.append(pl.BlockSpec((None, res, chunk, d),
                                         lambda n, g, r, cb=cb, last=blk // chunk - 1: (jnp.maximum(n - 1, 0), 0, last, cb + g)))
            args += [kv4, kv4]
    out = pl.pallas_call(
        functools.partial(_dil_attn_kernel, groups=DIL_GROUPS, heads_per_step=hs),
        out_shape=jax.ShapeDtypeStruct((n_sb, res, blk, heads * d), jnp.bfloat16),
        grid=(n_sb, kv_heads, rep // hs),
        in_specs=in_specs,
        out_specs=pl.BlockSpec((None, res, blk, hs * d), lambda n, g, r: (n, 0, 0, (g * rep) // hs + r)),
        scratch_shapes=[pltpu.VMEM((hs, res, blk, d), jnp.float32)] * (2 * ng),
        compiler_params=_params("parallel", "parallel", "arbitrary"),
        name="dilated_attention",
    )(*args)
    return out.reshape(s, heads * d)


def _gdn_layer(x, attn_norm, w_in, conv_w, a_log, dt_bias, out_norm, w_out):
    heads = a_log.shape[0]
    qk = heads * HEAD_DIM
    h = _rmsnorm(x, attn_norm)
    proj = _matmul(h, _bf(w_in[:, :4 * qk]), name="gdn_in_proj")
    gates = _gdn_gates(h, w_in[:, 4 * qk:], a_log, dt_bias)
    o = _gdn_core(proj, conv_w, gates, out_norm, heads)
    return _matmul(o, _bf(w_out), residual=x, name="gdn_out_proj")


def _shared_kv(x, kv_norm, w_kv, k_norm):
    n_dil = k_norm.shape[0]
    kvw = w_kv.shape[1] // (2 * n_dil)
    h = _rmsnorm_residue_major(x, kv_norm)
    gain = jnp.concatenate([jnp.tile(k_norm, (1, kvw // HEAD_DIM)), jnp.ones((n_dil, kvw), jnp.float32)], axis=1).reshape(1, -1)
    return _matmul_headnorm(h, _bf(w_kv), gain, scale=1.0, bn=kvw, norm_every=2, name="kv_proj")


def _dilated_layer(x, kv, attn_norm, w_q, q_norm, w_out):
    n_dil = q_norm.shape[0]
    qw = w_q.shape[1] // n_dil
    kvw = kv.shape[1] // (2 * n_dil)
    heads, kv_heads = qw // HEAD_DIM, kvw // HEAD_DIM
    h = _rmsnorm_residue_major(x, attn_norm)
    gain = jnp.tile(q_norm, (1, heads)).reshape(1, -1)
    q = _matmul_headnorm(h, _bf(w_q), gain, scale=HEAD_DIM ** -0.5, bn=min(MM_BN, qw), norm_every=1, name="q_proj")
    o = _dilated_attention(q, kv, heads, kv_heads)
    return _matmul_res_from_residue_major(o, _bf(w_out), x, name="attn_out_proj")


def _ffn_layer(x, norm_w, w_gate_up, w_down):
    hidden = w_down.shape[0]
    hidden_padded = -(-hidden // MM_BN) * MM_BN
    h = _rmsnorm(x, norm_w)
    act = _swiglu_up(h, _bf(w_gate_up[:, :hidden]), _bf(w_gate_up[:, hidden:]), hidden_padded)
    w_down_p = jnp.pad(_bf(w_down), ((0, hidden_padded - hidden), (0, 0)))
    return _matmul_ktiled_res(act, w_down_p, x, _pick_down_bk(hidden_padded), name="ffn_down")


def kernel(x, a_attn_norm, a_w_in, a_conv_w, a_a_log, a_dt_bias, a_out_norm, a_w_out, kv_norm, w_kv, k_norm,
           b_attn_norm, b_w_q, b_q_norm, b_w_out, ffn_norm, ffn_w_gate_up, ffn_w_down):
    batch, s, d = x.shape
    assert batch == 1
    x = x.reshape(s, d)
    depth = ffn_norm.shape[0]
    n_a = a_attn_norm.shape[0]
    kv = None
    for layer in range(depth):
        if layer < n_a:
            i = layer
            x = _gdn_layer(x, a_attn_norm[i], a_w_in[i], a_conv_w[i], a_a_log[i], a_dt_bias[i], a_out_norm[i], a_w_out[i])
        else:
            if layer == n_a:
                kv = _shared_kv(x, kv_norm, w_kv, k_norm)
            j = layer - n_a
            x = _dilated_layer(x, kv, b_attn_norm[j], b_w_q[j], b_q_norm[j], b_w_out[j])
        x = _ffn_layer(x, ffn_norm[layer], ffn_w_gate_up[layer], ffn_w_down[layer])
    return x.reshape(batch, s, d)
```

```python
import functools

import jax
import jax.numpy as jnp
from jax import lax
from jax.experimental import pallas as pl
from jax.experimental.pallas import tpu as pltpu

V7X_VMEM_LIMIT_BYTES = 56 * 1024 * 1024
LANE = 128
SUBLANE = 8

EPS = 1e-6
GDN_CONV = 4
DIL_GROUPS = ((128, 1), (512, 4), (2048, 16))
DIL_BLOCK = 128
HEAD_DIM = 128

MM_BM = 1024
MM_BN = 1024
SWIGLU_BN = 512
DOWN_BK_MAX = 3072
DOWN_BK_STEP = 256
NORM_BM = 256

GDN_CHUNK = 128
GDN_TB = 512
GDN_HEADS_PER_STEP = 4
GATES_BM = 512
ATT_RES = max(d for _, d in DIL_GROUPS)
ATT_SB = DIL_BLOCK * ATT_RES
ATT_HEADS_PER_STEP = 2
ATT_TILE_BATCH = 4
ATT_NORM_BI = 16
ATT_OUT_BN = 512


def _pick_down_bk(k_padded):
    best = DOWN_BK_STEP
    for bk in range(DOWN_BK_STEP, DOWN_BK_MAX + 1, DOWN_BK_STEP):
        if k_padded % bk == 0:
            best = bk
    return best


def _params(*semantics):
    return pltpu.CompilerParams(dimension_semantics=semantics, vmem_limit_bytes=V7X_VMEM_LIMIT_BYTES)


def _dot(a, b):
    return jnp.dot(a, b, preferred_element_type=jnp.float32)


def _dot_nt(a, b):
    return lax.dot_general(a, b, (((1,), (1,)), ((), ())), preferred_element_type=jnp.float32)


def _bf(x):
    return x.astype(jnp.bfloat16)


def _rmsnorm_kernel(x_ref, w_ref, o_ref):
    x = x_ref[...]
    y = x * lax.rsqrt(jnp.mean(x * x, axis=-1, keepdims=True) + EPS)
    o_ref[...] = (y * w_ref[...]).astype(o_ref.dtype)


def _rmsnorm(x, w, out_dtype=jnp.bfloat16):
    s, d = x.shape
    bm = min(NORM_BM, s)
    return pl.pallas_call(
        _rmsnorm_kernel,
        out_shape=jax.ShapeDtypeStruct((s, d), out_dtype),
        grid=(s // bm,),
        in_specs=[pl.BlockSpec((bm, d), lambda i: (i, 0)), pl.BlockSpec((1, d), lambda i: (0, 0))],
        out_specs=pl.BlockSpec((bm, d), lambda i: (i, 0)),
        compiler_params=_params("parallel"),
        name="rmsnorm",
    )(x, w.reshape(1, d))


def _mm_kernel(a_ref, w_ref, o_ref):
    o_ref[...] = _dot(a_ref[...], w_ref[...]).astype(o_ref.dtype)


def _mm_res_kernel(a_ref, w_ref, r_ref, o_ref):
    o_ref[...] = (r_ref[...] + _dot(a_ref[...], w_ref[...])).astype(o_ref.dtype)


def _matmul(a, w, residual=None, out_dtype=jnp.float32, name="matmul"):
    m, k = a.shape
    n = w.shape[1]
    bm, bn = min(MM_BM, m), min(MM_BN, n)
    in_specs = [pl.BlockSpec((bm, k), lambda i, j: (i, 0)), pl.BlockSpec((k, bn), lambda i, j: (0, j))]
    args = [a, w]
    kern = _mm_kernel
    if residual is not None:
        in_specs.append(pl.BlockSpec((bm, bn), lambda i, j: (i, j)))
        args.append(residual)
        kern = _mm_res_kernel
    return pl.pallas_call(
        kern,
        out_shape=jax.ShapeDtypeStruct((m, n), out_dtype),
        grid=(m // bm, n // bn),
        in_specs=in_specs,
        out_specs=pl.BlockSpec((bm, bn), lambda i, j: (i, j)),
        compiler_params=_params("parallel", "parallel"),
        name=name,
    )(*args)


def _mm_headnorm_kernel(a_ref, w_ref, g_ref, o_ref, *, scale, norm_every):
    y = _dot(a_ref[...], w_ref[...])
    bn = y.shape[1]

    def normed():
        for c in range(bn // LANE):
            ys = y[:, c * LANE:(c + 1) * LANE]
            inv = lax.rsqrt(jnp.mean(ys * ys, axis=-1, keepdims=True) + EPS)
            o_ref[:, c * LANE:(c + 1) * LANE] = (ys * inv * g_ref[:, c * LANE:(c + 1) * LANE] * scale).astype(o_ref.dtype)

    if norm_every == 1:
        normed()
    else:
        j = pl.program_id(1)

        @pl.when(j % norm_every == 0)
        def _():
            normed()

        @pl.when(j % norm_every != 0)
        def _():
            o_ref[...] = y.astype(o_ref.dtype)


def _matmul_headnorm(a, w, gain_row, *, scale, bn, norm_every, name):
    m, k = a.shape
    n = w.shape[1]
    bm = min(MM_BM, m)
    return pl.pallas_call(
        functools.partial(_mm_headnorm_kernel, scale=scale, norm_every=norm_every),
        out_shape=jax.ShapeDtypeStruct((m, n), jnp.float32),
        grid=(m // bm, n // bn),
        in_specs=[
            pl.BlockSpec((bm, k), lambda i, j: (i, 0)),
            pl.BlockSpec((k, bn), lambda i, j: (0, j)),
            pl.BlockSpec((1, bn), lambda i, j: (0, j)),
        ],
        out_specs=pl.BlockSpec((bm, bn), lambda i, j: (i, j)),
        compiler_params=_params("parallel", "parallel"),
        name=name,
    )(a, w, gain_row)


def _swiglu_up_kernel(h_ref, wg_ref, wu_ref, o_ref, *, hidden):
    h = h_ref[...]
    g = _dot(h, wg_ref[...])
    u = _dot(h, wu_ref[...])
    act = g * jax.nn.sigmoid(g) * u
    bn = act.shape[1]
    col = pl.program_id(1) * bn + lax.broadcasted_iota(jnp.int32, act.shape, 1)
    o_ref[...] = jnp.where(col < hidden, act, 0.0).astype(o_ref.dtype)


def _swiglu_up(h, wg, wu, hidden_padded):
    m, k = h.shape
    hidden = wg.shape[1]
    bm, bn = min(MM_BM, m), SWIGLU_BN
    return pl.pallas_call(
        functools.partial(_swiglu_up_kernel, hidden=hidden),
        out_shape=jax.ShapeDtypeStruct((m, hidden_padded), jnp.bfloat16),
        grid=(m // bm, hidden_padded // bn),
        in_specs=[
            pl.BlockSpec((bm, k), lambda i, j: (i, 0)),
            pl.BlockSpec((k, bn), lambda i, j: (0, j)),
            pl.BlockSpec((k, bn), lambda i, j: (0, j)),
        ],
        out_specs=pl.BlockSpec((bm, bn), lambda i, j: (i, j)),
        compiler_params=_params("parallel", "parallel"),
        name="swiglu_up",
    )(h, wg, wu)


def _mm_acc_res_kernel(a_ref, w_ref, r_ref, o_ref, acc_ref):
    kk = pl.program_id(2)

    @pl.when(kk == 0)
    def _():
        acc_ref[...] = r_ref[...]

    acc_ref[...] += _dot(a_ref[...], w_ref[...])

    @pl.when(kk == pl.num_programs(2) - 1)
    def _():
        o_ref[...] = acc_ref[...]


def _matmul_ktiled_res(a, w, residual, bk, name):
    m, k = a.shape
    n = w.shape[1]
    bm, bn = min(MM_BM, m), min(MM_BN, n)
    return pl.pallas_call(
        _mm_acc_res_kernel,
        out_shape=jax.ShapeDtypeStruct((m, n), jnp.float32),
        grid=(m // bm, n // bn, k // bk),
        in_specs=[
            pl.BlockSpec((bm, bk), lambda i, j, kk: (i, kk)),
            pl.BlockSpec((bk, bn), lambda i, j, kk: (kk, j)),
            pl.BlockSpec((bm, bn), lambda i, j, kk: (i, j)),
        ],
        out_specs=pl.BlockSpec((bm, bn), lambda i, j, kk: (i, j)),
        scratch_shapes=[pltpu.VMEM((bm, bn), jnp.float32)],
        compiler_params=_params("parallel", "parallel", "arbitrary"),
        name=name,
    )(a, w, residual)


def _gdn_gates_kernel(h_ref, w_ref, alog_ref, dt_ref, o_ref):
    nh = alog_ref.shape[0]
    yt = _dot(h_ref[...], w_ref[...]).T
    o_ref[0] = jax.nn.sigmoid(yt[:nh])
    x = yt[nh:2 * nh] + dt_ref[...]
    softplus = jnp.maximum(x, 0.0) + jnp.log1p(jnp.exp(-jnp.abs(x)))
    g = -jnp.exp(alog_ref[...]) * softplus
    lane = lax.broadcasted_iota(jnp.int32, (nh, GDN_CHUNK), 1)
    for c in range(g.shape[1] // GDN_CHUNK):
        acc = g[:, c * GDN_CHUNK:(c + 1) * GDN_CHUNK]
        shift = 1
        while shift < GDN_CHUNK:
            acc = acc + jnp.where(lane >= shift, pltpu.roll(acc, shift, 1), 0.0)
            shift *= 2
        o_ref[1, :, c * GDN_CHUNK:(c + 1) * GDN_CHUNK] = acc


def _gdn_gates(h, w_gate, a_log, dt_bias):
    s, k = h.shape
    nh = a_log.shape[0]
    bm = min(GATES_BM, s)
    w_pad = jnp.pad(_bf(w_gate), ((0, 0), (0, LANE - 2 * nh)))
    out = pl.pallas_call(
        _gdn_gates_kernel,
        out_shape=jax.ShapeDtypeStruct((2, nh, s), jnp.float32),
        grid=(s // bm,),
        in_specs=[
            pl.BlockSpec((bm, k), lambda i: (i, 0)),
            pl.BlockSpec((k, LANE), lambda i: (0, 0)),
            pl.BlockSpec((nh, 1), lambda i: (0, 0)),
            pl.BlockSpec((nh, 1), lambda i: (0, 0)),
        ],
        out_specs=pl.BlockSpec((2, nh, bm), lambda i: (0, 0, i)),
        compiler_params=_params("parallel"),
        name="gdn_gates",
    )(h, w_pad, a_log.reshape(nh, 1), dt_bias.reshape(nh, 1))
    return out.reshape(2, nh, 1, s)


def _gdn_core_kernel(q_ref, k_ref, v_ref, z_ref, cwq_ref, cwk_ref, cwv_ref, gate_ref, gate_prev_ref, onorm_ref, o_ref,
                     state_ref, extq_ref, extk_ref, extv_ref, kq_ref, bo_ref):
    tb = q_ref.shape[0]
    hp = q_ref.shape[1] // HEAD_DIM
    c = GDN_CHUNK
    nt = tb // c
    step = pl.program_id(1)
    slot_w = step % 2
    slot_r = 1 - slot_w

    @pl.when(step == 0)
    def _():
        state_ref[...] = jnp.zeros_like(state_ref)
        for ext in (extq_ref, extk_ref, extv_ref):
            ext[0:SUBLANE, :] = jnp.zeros((SUBLANE, hp * HEAD_DIM), jnp.float32)
        kq_ref[1] = jnp.zeros(kq_ref.shape[1:], kq_ref.dtype)
        bo_ref[1] = jnp.zeros(bo_ref.shape[1:], bo_ref.dtype)

    tiles = [(hh, n) for n in range(nt) for hh in range(hp)]
    state = [state_ref[hh] for hh in range(hp)]
    for n in range(nt):
        sl = slice(n * c, (n + 1) * c)
        for hh in range(hp):
            cols = slice(hh * HEAD_DIM, (hh + 1) * HEAD_DIM)
            ss = _dot(kq_ref[slot_r, hh, n], _bf(state[hh]))
            o = ss[c:] + bo_ref[slot_r, hh, n, c:2 * c, :]
            g_last = jnp.exp(gate_prev_ref[1, hh, :, (n + 1) * c - 1:(n + 1) * c])
            state[hh] = state[hh] * g_last - ss[:c] + bo_ref[slot_r, hh, n, 0:c, :]
            o = o * lax.rsqrt(jnp.mean(o * o, axis=-1, keepdims=True) + EPS) * onorm_ref[...]
            z = z_ref[sl, cols]
            o_ref[sl, cols] = (o * (z * jax.nn.sigmoid(z))).astype(o_ref.dtype)
    for hh in range(hp):
        state_ref[hh] = state[hh]

    def conv_silu(x_ref, ext, cw_ref):
        ext[SUBLANE:SUBLANE + tb, :] = x_ref[...]
        acc = ext[pl.ds(SUBLANE, tb), :] * cw_ref[GDN_CONV - 1:GDN_CONV, :]
        for back in range(1, GDN_CONV):
            acc = acc + ext[pl.ds(SUBLANE - back, tb), :] * cw_ref[GDN_CONV - 1 - back:GDN_CONV - back, :]
        ext[0:SUBLANE, :] = ext[tb:tb + SUBLANE, :]
        return acc * jax.nn.sigmoid(acc)

    def l2norm(x):
        return x * lax.rsqrt(jnp.sum(x * x, axis=-1, keepdims=True) + EPS)

    def head_cols(x, hh):
        return x[:, hh * HEAD_DIM:(hh + 1) * HEAD_DIM]

    q_conv = conv_silu(q_ref, extq_ref, cwq_ref)
    k_conv = conv_silu(k_ref, extk_ref, cwk_ref)
    v_conv = conv_silu(v_ref, extv_ref, cwv_ref)
    q_all = [l2norm(head_cols(q_conv, hh)) * (HEAD_DIM ** -0.5) for hh in range(hp)]
    k_all = [l2norm(head_cols(k_conv, hh)) for hh in range(hp)]
    v_all = [head_cols(v_conv, hh) for hh in range(hp)]

    row = lax.broadcasted_iota(jnp.int32, (c, c), 0)
    col = lax.broadcasted_iota(jnp.int32, (c, c), 1)
    causal = row >= col
    strict = row > col
    eye = jnp.where(row == col, 1.0, 0.0)
    pair_masks = [((row >> (l + 1)) == (col >> (l + 1))) & ((row >> l) != (col >> l)) for l in range(c.bit_length() - 1)]

    q_t, k_t, v_t, kb_t, decay_t, egc_t, beta_t, kdec_t, a_t = {}, {}, {}, {}, {}, {}, {}, {}, {}
    for t in tiles:
        hh, n = t
        sl = slice(n * c, (n + 1) * c)
        q_t[t], k_t[t], v_t[t] = q_all[hh][sl], k_all[hh][sl], v_all[hh][sl]
        gc_row = gate_ref[1, hh, :, sl]
        gc_r = jnp.broadcast_to(gc_row, (c, c))
        gc_c = gc_r.T
        beta_t[t] = jnp.broadcast_to(gate_ref[0, hh, :, sl], (c, c)).T
        decay_t[t] = jnp.where(causal, jnp.exp(jnp.where(causal, gc_c - gc_r, 0.0)), 0.0)
        egc_t[t] = jnp.exp(gc_c)
        kdec_t[t] = k_t[t] * jnp.exp(gc_row[:, c - 1:c] - gc_c)
        kb_t[t] = k_t[t] * beta_t[t]
        a_t[t] = jnp.where(strict, _dot_nt(_bf(kb_t[t]), _bf(k_t[t])) * decay_t[t], 0.0)
    t_t = {t: eye - jnp.where(pair_masks[0], a_t[t], 0.0) for t in tiles}
    a16 = {t: _bf(a_t[t]) for t in tiles}
    for off_mask in pair_masks[1:]:
        t16 = {t: _bf(t_t[t]) for t in tiles}
        ta = {t: _dot(t16[t], a16[t]) for t in tiles}
        t_t = {t: jnp.where(off_mask, t_t[t] - _dot(_bf(ta[t]), t16[t]), t_t[t]) for t in tiles}
    wu16 = {t: _bf(_dot(_bf(t_t[t]), _bf(jnp.concatenate([kb_t[t] * egc_t[t], v_t[t] * beta_t[t]], axis=1))))
            for t in tiles}
    attn16 = {t: _bf(jnp.where(causal, _dot_nt(_bf(q_t[t]), _bf(k_t[t])) * decay_t[t], 0.0)) for t in tiles}
    kw_ku = {t: _dot(_bf(kdec_t[t].T), wu16[t]) for t in tiles}
    aw_au = {t: _dot(attn16[t], wu16[t]) for t in tiles}
    for t in tiles:
        hh, n = t
        kq_ref[slot_w, hh, n, 0:c, :] = _bf(kw_ku[t][:, :HEAD_DIM])
        kq_ref[slot_w, hh, n, c:2 * c, :] = _bf(q_t[t] * egc_t[t] - aw_au[t][:, :HEAD_DIM])
        bo_ref[slot_w, hh, n, 0:c, :] = kw_ku[t][:, HEAD_DIM:]
        bo_ref[slot_w, hh, n, c:2 * c, :] = aw_au[t][:, HEAD_DIM:]


def _gdn_core(proj, conv_w, gates, out_norm, heads):
    s = proj.shape[0]
    tb = min(GDN_TB, s)
    d = HEAD_DIM
    hp = GDN_HEADS_PER_STEP
    assert heads % hp == 0
    n_hb = heads // hp
    n_tb = s // tb
    cur = lambda t: jnp.minimum(t, n_tb - 1)
    prev = lambda t: jnp.maximum(t - 1, 0)
    col = lambda off: (lambda h, t: (cur(t), off * n_hb + h))
    cw = lambda off: (lambda h, t: (0, off * n_hb + h))
    nt = tb // GDN_CHUNK
    return pl.pallas_call(
        _gdn_core_kernel,
        out_shape=jax.ShapeDtypeStruct((s, heads * d), jnp.bfloat16),
        grid=(n_hb, n_tb + 1),
        in_specs=[
            pl.BlockSpec((tb, hp * d), col(0)), pl.BlockSpec((tb, hp * d), col(1)), pl.BlockSpec((tb, hp * d), col(2)),
            pl.BlockSpec((tb, hp * d), lambda h, t: (prev(t), 3 * n_hb + h)),
            pl.BlockSpec((GDN_CONV, hp * d), cw(0)), pl.BlockSpec((GDN_CONV, hp * d), cw(1)),
            pl.BlockSpec((GDN_CONV, hp * d), cw(2)),
            pl.BlockSpec((2, hp, 1, tb), lambda h, t: (0, h, 0, cur(t))),
            pl.BlockSpec((2, hp, 1, tb), lambda h, t: (0, h, 0, prev(t))),
            pl.BlockSpec((1, d), lambda h, t: (0, 0)),
        ],
        out_specs=pl.BlockSpec((tb, hp * d), lambda h, t: (prev(t), h)),
        scratch_shapes=[pltpu.VMEM((hp, d, d), jnp.float32)] + [pltpu.VMEM((tb + SUBLANE, hp * d), jnp.float32)] * 3
        + [pltpu.VMEM((2, hp, nt, 2 * GDN_CHUNK, d), jnp.bfloat16), pltpu.VMEM((2, hp, nt, 2 * GDN_CHUNK, d), jnp.float32)],
        compiler_params=_params("parallel", "arbitrary"),
        name="gdn_core",
    )(proj, proj, proj, proj, conv_w, conv_w, conv_w, gates, gates, out_norm.reshape(1, d))


def _rmsnorm_rm_kernel(x_ref, w_ref, o_ref):
    x = x_ref[...]
    y = _bf(x * lax.rsqrt(jnp.mean(x * x, axis=-1, keepdims=True) + EPS) * w_ref[...])
    n = x.shape[0]
    bi = n // ATT_RES
    row = lax.broadcasted_iota(jnp.int32, (n, n), 0)
    col = lax.broadcasted_iota(jnp.int32, (n, n), 1)
    perm = _bf(jnp.where(col == ATT_RES * (row % bi) + row // bi, 1.0, 0.0))
    out = _dot(perm, y)
    for r in range(ATT_RES):
        o_ref[r] = out[r * bi:(r + 1) * bi].astype(o_ref.dtype)


def _rmsnorm_residue_major(x, w):
    s, d = x.shape
    n_sb = s // ATT_SB
    bi = ATT_NORM_BI
    steps = DIL_BLOCK // bi
    out = pl.pallas_call(
        _rmsnorm_rm_kernel,
        out_shape=jax.ShapeDtypeStruct((n_sb, ATT_RES, DIL_BLOCK, d), jnp.bfloat16),
        grid=(n_sb, steps),
        in_specs=[pl.BlockSpec((bi * ATT_RES, d), lambda n, i: (n * steps + i, 0)),
                  pl.BlockSpec((1, d), lambda n, i: (0, 0))],
        out_specs=pl.BlockSpec((None, ATT_RES, bi, d), lambda n, i: (n, 0, i, 0)),
        compiler_params=_params("parallel", "parallel"),
        name="rmsnorm_residue_major",
    )(x, w.reshape(1, d))
    return out.reshape(s, d)


def _mm_res_natural_kernel(a_ref, w_ref, r_ref, o_ref, anat_ref):
    @pl.when(pl.program_id(1) == 0)
    def _():
        bi = ATT_NORM_BI
        n = bi * ATT_RES
        row = lax.broadcasted_iota(jnp.int32, (n, n), 0)
        col = lax.broadcasted_iota(jnp.int32, (n, n), 1)
        perm = _bf(jnp.where(col == bi * (row % ATT_RES) + row // ATT_RES, 1.0, 0.0))
        for grp in range(DIL_BLOCK // bi):
            gathered = jnp.concatenate(
                [a_ref[r * DIL_BLOCK + grp * bi:r * DIL_BLOCK + (grp + 1) * bi, :] for r in range(ATT_RES)], axis=0)
            anat_ref[grp * n:(grp + 1) * n, :] = _dot(perm, gathered).astype(anat_ref.dtype)

    o_ref[...] = r_ref[...] + _dot(anat_ref[...], w_ref[...])


def _matmul_res_from_residue_major(a, w, residual, name):
    m, k = a.shape
    n = w.shape[1]
    bn = min(ATT_OUT_BN, n)
    return pl.pallas_call(
        _mm_res_natural_kernel,
        out_shape=jax.ShapeDtypeStruct((m, n), jnp.float32),
        grid=(m // ATT_SB, n // bn),
        in_specs=[pl.BlockSpec((ATT_SB, k), lambda i, j: (i, 0)), pl.BlockSpec((k, bn), lambda i, j: (0, j)),
                  pl.BlockSpec((ATT_SB, bn), lambda i, j: (i, j))],
        out_specs=pl.BlockSpec((ATT_SB, bn), lambda i, j: (i, j)),
        scratch_shapes=[pltpu.VMEM((ATT_SB, k), a.dtype)],
        compiler_params=_params("parallel", "arbitrary"),
        name=name,
    )(a, w, residual)


def _dil_attn_kernel(*refs, groups, heads_per_step):
    ng = len(groups)
    hs = heads_per_step
    q_refs = refs[:ng]
    kv_refs = refs[ng:5 * ng]
    o_ref = refs[5 * ng]
    og_refs = refs[5 * ng + 1:6 * ng + 1]
    lse_refs = refs[6 * ng + 1:7 * ng + 1]
    blk = DIL_BLOCK
    not_first = pl.program_id(0) > 0
    row = lax.broadcasted_iota(jnp.int32, (hs * blk, 2 * blk), 0) % blk
    col = lax.broadcasted_iota(jnp.int32, (hs * blk, 2 * blk), 1)
    is_prev = col < blk
    col = col % blk
    ones16 = jnp.ones((2 * blk, HEAD_DIM), jnp.bfloat16)
    neg_inf = jnp.float32(-jnp.inf)

    for gi, (window, dil) in enumerate(groups):
        span = window // dil
        per = ATT_RES // dil
        chunk = blk // per
        jq = per * (row % chunk) + row // chunk
        jk = per * (col % chunk) + col // chunk
        own_ok = (~is_prev) & (jq - jk >= 0) & (jq - jk <= span)
        prev_ok = is_prev & (jq + blk - jk <= span)
        mask = own_ok | prev_ok
        mask_first = own_ok | (prev_ok & not_first)
        k_own, k_prev, v_own, v_prev = kv_refs[4 * gi:4 * gi + 4]

        def tile_rows(ref, r_d, lo, cols=slice(None)):
            return jnp.concatenate([ref[r_d + dil * m, lo:lo + chunk, cols] for m in range(per)], axis=0)

        tiles = [(r_d, b) for r_d in range(dil) for b in range(per)]
        for t0 in range(0, len(tiles), ATT_TILE_BATCH):
            batch = tiles[t0:t0 + ATT_TILE_BATCH]
            q16, k16, v16 = {}, {}, {}
            for t in batch:
                r_d, b = t
                lo = chunk * b
                q16[t] = _bf(jnp.concatenate(
                    [tile_rows(q_refs[gi], r_d, lo, slice(h * HEAD_DIM, (h + 1) * HEAD_DIM)) for h in range(hs)], axis=0))
                if b > 0:
                    kp, vp = tile_rows(k_own, r_d, lo - chunk), tile_rows(v_own, r_d, lo - chunk)
                else:
                    kp, vp = tile_rows(k_prev, r_d, 0), tile_rows(v_prev, r_d, 0)
                k16[t] = _bf(jnp.concatenate([kp, tile_rows(k_own, r_d, lo)], axis=0))
                v16[t] = jnp.concatenate([_bf(jnp.concatenate([vp, tile_rows(v_own, r_d, lo)], axis=0)), ones16], axis=1)
            s = {t: jnp.where(mask if t[1] > 0 else mask_first, _dot_nt(q16[t], k16[t]), neg_inf) for t in batch}
            mx = {t: jnp.max(jnp.maximum(s[t][:, :blk], s[t][:, blk:]), axis=-1, keepdims=True) for t in batch}
            e16 = {t: _bf(jnp.exp(s[t] - mx[t])) for t in batch}
            pv = {t: _dot(e16[t], v16[t]) for t in batch}
            for t in batch:
                r_d, b = t
                den = pv[t][:, HEAD_DIM:]
                o_t = pv[t][:, :HEAD_DIM] / den
                lse_t = mx[t] + jnp.log(den)
                for h in range(hs):
                    for m in range(per):
                        src = slice(h * blk + m * chunk, h * blk + (m + 1) * chunk)
                        og_refs[gi][h, r_d + dil * m, chunk * b:chunk * (b + 1), :] = o_t[src]
                        lse_refs[gi][h, r_d + dil * m, chunk * b:chunk * (b + 1), :] = lse_t[src]

    for h in range(hs):
        lse_max = lse_refs[0][h]
        for gi in range(1, ng):
            lse_max = jnp.maximum(lse_max, lse_refs[gi][h])
        num = jnp.zeros(lse_max.shape, jnp.float32)
        den = jnp.zeros(lse_max.shape, jnp.float32)
        for gi in range(ng):
            wgt = jnp.exp(lse_refs[gi][h] - lse_max)
            num = num + wgt * og_refs[gi][h]
            den = den + wgt
        o_ref[:, :, h * HEAD_DIM:(h + 1) * HEAD_DIM] = (num / den).astype(o_ref.dtype)


def _dilated_attention(q, kv, heads, kv_heads):
    s = q.shape[0]
    ng = len(DIL_GROUPS)
    res, blk, d, hs = ATT_RES, DIL_BLOCK, HEAD_DIM, ATT_HEADS_PER_STEP
    assert s % ATT_SB == 0 and all(w // dil <= blk and res % dil == 0 for w, dil in DIL_GROUPS)
    n_sb = s // ATT_SB
    rep = heads // kv_heads
    assert rep % hs == 0
    q4 = q.reshape(n_sb, res, blk, q.shape[1])
    kv4 = kv.reshape(n_sb, res, blk, kv.shape[1])
    in_specs = [pl.BlockSpec((None, res, blk, hs * d),
                             lambda n, g, r, gi=gi: (n, 0, 0, (gi * heads + g * rep) // hs + r)) for gi in range(ng)]
    args = [q4] * ng
    for gi, (_, dil) in enumerate(DIL_GROUPS):
        chunk = blk // (res // dil)
        for part in range(2):
            cb = (gi * 2 + part) * kv_heads
            in_specs.append(pl.BlockSpec((None, res, blk, d), lambda n, g, r, cb=cb: (n, 0, 0, cb + g)))
            in_specs.append(pl.BlockSpec((None, res, chunk, d),
                                         lambda n, g, r, cb=cb, last=blk // chunk - 1: (jnp.maximum(n - 1, 0), 0, last, cb + g)))
            args += [kv4, kv4]
    out = pl.pallas_call(
        functools.partial(_dil_attn_kernel, groups=DIL_GROUPS, heads_per_step=hs),
        out_shape=jax.ShapeDtypeStruct((n_sb, res, blk, heads * d), jnp.bfloat16),
        grid=(n_sb, kv_heads, rep // hs),
        in_specs=in_specs,
        out_specs=pl.BlockSpec((None, res, blk, hs * d), lambda n, g, r: (n, 0, 0, (g * rep) // hs + r)),
        scratch_shapes=[pltpu.VMEM((hs, res, blk, d), jnp.float32)] * (2 * ng),
        compiler_params=_params("parallel", "parallel", "arbitrary"),
        name="dilated_attention",
    )(*args)
    return out.reshape(s, heads * d)


def _gdn_layer(x, attn_norm, w_in, conv_w, a_log, dt_bias, out_norm, w_out):
    heads = a_log.shape[0]
    qk = heads * HEAD_DIM
    h = _rmsnorm(x, attn_norm)
    proj = _matmul(h, _bf(w_in[:, :4 * qk]), name="gdn_in_proj")
    gates = _gdn_gates(h, w_in[:, 4 * qk:], a_log, dt_bias)
    o = _gdn_core(proj, conv_w, gates, out_norm, heads)
    return _matmul(o, _bf(w_out), residual=x, name="gdn_out_proj")


def _shared_kv(x, kv_norm, w_kv, k_norm):
    n_dil = k_norm.shape[0]
    kvw = w_kv.shape[1] // (2 * n_dil)
    h = _rmsnorm_residue_major(x, kv_norm)
    gain = jnp.concatenate([jnp.tile(k_norm, (1, kvw // HEAD_DIM)), jnp.ones((n_dil, kvw), jnp.float32)], axis=1).reshape(1, -1)
    return _matmul_headnorm(h, _bf(w_kv), gain, scale=1.0, bn=kvw, norm_every=2, name="kv_proj")


def _dilated_layer(x, kv, attn_norm, w_q, q_norm, w_out):
    n_dil = q_norm.shape[0]
    qw = w_q.shape[1] // n_dil
    kvw = kv.shape[1] // (2 * n_dil)
    heads, kv_heads = qw // HEAD_DIM, kvw // HEAD_DIM
    h = _rmsnorm_residue_major(x, attn_norm)
    gain = jnp.tile(q_norm, (1, heads)).reshape(1, -1)
    q = _matmul_headnorm(h, _bf(w_q), gain, scale=HEAD_DIM ** -0.5, bn=min(MM_BN, qw), norm_every=1, name="q_proj")
    o = _dilated_attention(q, kv, heads, kv_heads)
    return _matmul_res_from_residue_major(o, _bf(w_out), x, name="attn_out_proj")


def _ffn_layer(x, norm_w, w_gate_up, w_down):
    hidden = w_down.shape[0]
    hidden_padded = -(-hidden // MM_BN) * MM_BN
    h = _rmsnorm(x, norm_w)
    act = _swiglu_up(h, _bf(w_gate_up[:, :hidden]), _bf(w_gate_up[:, hidden:]), hidden_padded)
    w_down_p = jnp.pad(_bf(w_down), ((0, hidden_padded - hidden), (0, 0)))
    return _matmul_ktiled_res(act, w_down_p, x, _pick_down_bk(hidden_padded), name="ffn_down")


def kernel(x, a_attn_norm, a_w_in, a_conv_w, a_a_log, a_dt_bias, a_out_norm, a_w_out, kv_norm, w_kv, k_norm,
           b_attn_norm, b_w_q, b_q_norm, b_w_out, ffn_norm, ffn_w_gate_up, ffn_w_down):
    batch, s, d = x.shape
    assert batch == 1
    x = x.reshape(s, d)
    depth = ffn_norm.shape[0]
    n_a = a_attn_norm.shape[0]
    kv = None
    for layer in range(depth):
        if layer < n_a:
            i = layer
            x = _gdn_layer(x, a_attn_norm[i], a_w_in[i], a_conv_w[i], a_a_log[i], a_dt_bias[i], a_out_norm[i], a_w_out[i])
        else:
            if layer == n_a:
                kv = _shared_kv(x, kv_norm, w_kv, k_norm)
            j = layer - n_a
            x = _dilated_layer(x, kv, b_attn_norm[j], b_w_q[j], b_q_norm[j], b_w_out[j])
        x = _ffn_layer(x, ffn_norm[layer], ffn_w_gate_up[layer], ffn_w_down[layer])
    return x.reshape(batch, s, d)
```

```python
import functools

import jax
import jax.numpy as jnp
from jax import lax
from jax.experimental import pallas as pl
from jax.experimental.pallas import tpu as pltpu

V7X_VMEM_LIMIT_BYTES = 56 * 1024 * 1024
LANE = 128
SUBLANE = 8

EPS = 1e-6
GDN_CONV = 4
DIL_GROUPS = ((128, 1), (512, 4), (2048, 16))
DIL_BLOCK = 128
HEAD_DIM = 128

MM_BM = 1024
MM_BN = 1024
SWIGLU_BN = 512
FFN_DOWN_BM = 512
FFN_DOWN_BN = 512
GDN_OUT_BN = 512
NORM_BM = 256

GDN_CHUNK = 128
GDN_TB = 512
GDN_HEADS_PER_STEP = 4
GATES_BM = 512
ATT_RES = max(d for _, d in DIL_GROUPS)
ATT_SB = DIL_BLOCK * ATT_RES
ATT_HEADS_PER_STEP = 2
ATT_TILE_BATCH = 4
ATT_NORM_BI = 16
ATT_OUT_BN = 512


def _params(*semantics):
    return pltpu.CompilerParams(dimension_semantics=semantics, vmem_limit_bytes=V7X_VMEM_LIMIT_BYTES)


def _dot(a, b):
    return jnp.dot(a, b, preferred_element_type=jnp.float32)


def _dot_nt(a, b):
    return lax.dot_general(a, b, (((1,), (1,)), ((), ())), preferred_element_type=jnp.float32)


def _bf(x):
    return x.astype(jnp.bfloat16)


def _rmsnorm_kernel(x_ref, w_ref, o_ref):
    x = x_ref[...]
    y = x * lax.rsqrt(jnp.mean(x * x, axis=-1, keepdims=True) + EPS)
    o_ref[...] = (y * w_ref[...]).astype(o_ref.dtype)


def _rmsnorm(x, w, out_dtype=jnp.bfloat16):
    s, d = x.shape
    bm = min(NORM_BM, s)
    return pl.pallas_call(
        _rmsnorm_kernel,
        out_shape=jax.ShapeDtypeStruct((s, d), out_dtype),
        grid=(s // bm,),
        in_specs=[pl.BlockSpec((bm, d), lambda i: (i, 0)), pl.BlockSpec((1, d), lambda i: (0, 0))],
        out_specs=pl.BlockSpec((bm, d), lambda i: (i, 0)),
        compiler_params=_params("parallel"),
        name="rmsnorm",
    )(x, w.reshape(1, d))


def _mm_kernel(a_ref, w_ref, o_ref):
    o_ref[...] = _dot(a_ref[...], w_ref[...]).astype(o_ref.dtype)


def _mm_res_kernel(a_ref, w_ref, r_ref, o_ref):
    o_ref[...] = (r_ref[...] + _dot(a_ref[...], w_ref[...])).astype(o_ref.dtype)


def _mm_res_dual_kernel(a_ref, w_ref, r_ref, o_ref, o16_ref):
    x = r_ref[...] + _dot(a_ref[...], w_ref[...])
    o_ref[...] = x
    o16_ref[...] = x.astype(o16_ref.dtype)


def _matmul(a, w, residual=None, out_dtype=jnp.float32, name="matmul", bm=MM_BM, bn=MM_BN, bf16_copy=False):
    m, k = a.shape
    n = w.shape[1]
    bm, bn = min(bm, m), min(bn, n)
    tile = pl.BlockSpec((bm, bn), lambda i, j: (i, j))
    in_specs = [pl.BlockSpec((bm, k), lambda i, j: (i, 0)), pl.BlockSpec((k, bn), lambda i, j: (0, j))]
    args = [a, w]
    kern = _mm_kernel
    out_shape, out_specs = jax.ShapeDtypeStruct((m, n), out_dtype), tile
    if residual is not None:
        in_specs.append(tile)
        args.append(residual)
        kern = _mm_res_kernel
        if bf16_copy:
            kern = _mm_res_dual_kernel
            out_shape, out_specs = (out_shape, jax.ShapeDtypeStruct((m, n), jnp.bfloat16)), (tile, tile)
    return pl.pallas_call(
        kern,
        out_shape=out_shape,
        grid=(m // bm, n // bn),
        in_specs=in_specs,
        out_specs=out_specs,
        compiler_params=_params("parallel", "parallel"),
        name=name,
    )(*args)


def _mm_headnorm_kernel(a_ref, w_ref, g_ref, o_ref, *, scale, norm_every):
    y = _dot(a_ref[...], w_ref[...])
    bn = y.shape[1]

    def normed():
        for c in range(bn // LANE):
            ys = y[:, c * LANE:(c + 1) * LANE]
            inv = lax.rsqrt(jnp.mean(ys * ys, axis=-1, keepdims=True) + EPS)
            o_ref[:, c * LANE:(c + 1) * LANE] = (ys * inv * g_ref[:, c * LANE:(c + 1) * LANE] * scale).astype(o_ref.dtype)

    if norm_every == 1:
        normed()
    else:
        j = pl.program_id(1)

        @pl.when(j % norm_every == 0)
        def _():
            normed()

        @pl.when(j % norm_every != 0)
        def _():
            o_ref[...] = y.astype(o_ref.dtype)


def _matmul_headnorm(a, w, gain_row, *, scale, bn, norm_every, name):
    m, k = a.shape
    n = w.shape[1]
    bm = min(MM_BM, m)
    return pl.pallas_call(
        functools.partial(_mm_headnorm_kernel, scale=scale, norm_every=norm_every),
        out_shape=jax.ShapeDtypeStruct((m, n), jnp.float32),
        grid=(m // bm, n // bn),
        in_specs=[
            pl.BlockSpec((bm, k), lambda i, j: (i, 0)),
            pl.BlockSpec((k, bn), lambda i, j: (0, j)),
            pl.BlockSpec((1, bn), lambda i, j: (0, j)),
        ],
        out_specs=pl.BlockSpec((bm, bn), lambda i, j: (i, j)),
        compiler_params=_params("parallel", "parallel"),
        name=name,
    )(a, w, gain_row)


def _swiglu_up_kernel(x_ref, wg_ref, wu_ref, o_ref, *, hidden):
    x = x_ref[...]
    xf = x.astype(jnp.float32)
    inv = lax.rsqrt(jnp.mean(xf * xf, axis=-1, keepdims=True) + EPS)
    g = _dot(x, wg_ref[...]) * inv
    u = _dot(x, wu_ref[...]) * inv
    act = g * jax.nn.sigmoid(g) * u
    bn = act.shape[1]
    col = pl.program_id(1) * bn + lax.broadcasted_iota(jnp.int32, act.shape, 1)
    o_ref[...] = jnp.where(col < hidden, act, 0.0).astype(o_ref.dtype)


def _swiglu_up(h, wg, wu, hidden_padded):
    m, k = h.shape
    hidden = wg.shape[1]
    bm, bn = min(MM_BM, m), SWIGLU_BN
    return pl.pallas_call(
        functools.partial(_swiglu_up_kernel, hidden=hidden),
        out_shape=jax.ShapeDtypeStruct((m, hidden_padded), jnp.bfloat16),
        grid=(m // bm, hidden_padded // bn),
        in_specs=[
            pl.BlockSpec((bm, k), lambda i, j: (i, 0)),
            pl.BlockSpec((k, bn), lambda i, j: (0, j)),
            pl.BlockSpec((k, bn), lambda i, j: (0, j)),
        ],
        out_specs=pl.BlockSpec((bm, bn), lambda i, j: (i, j)),
        compiler_params=_params("parallel", "parallel"),
        name="swiglu_up",
    )(h, wg, wu)


def _gdn_gates_kernel(h_ref, w_ref, alog_ref, dt_ref, o_ref):
    nh = alog_ref.shape[0]
    yt = _dot(h_ref[...], w_ref[...]).T
    o_ref[0] = jax.nn.sigmoid(yt[:nh])
    x = yt[nh:2 * nh] + dt_ref[...]
    softplus = jnp.maximum(x, 0.0) + jnp.log1p(jnp.exp(-jnp.abs(x)))
    g = -jnp.exp(alog_ref[...]) * softplus
    lane = lax.broadcasted_iota(jnp.int32, (nh, GDN_CHUNK), 1)
    for c in range(g.shape[1] // GDN_CHUNK):
        acc = g[:, c * GDN_CHUNK:(c + 1) * GDN_CHUNK]
        shift = 1
        while shift < GDN_CHUNK:
            acc = acc + jnp.where(lane >= shift, pltpu.roll(acc, shift, 1), 0.0)
            shift *= 2
        o_ref[1, :, c * GDN_CHUNK:(c + 1) * GDN_CHUNK] = acc


def _gdn_gates(h, w_gate, a_log, dt_bias):
    s, k = h.shape
    nh = a_log.shape[0]
    bm = min(GATES_BM, s)
    w_pad = jnp.pad(_bf(w_gate), ((0, 0), (0, LANE - 2 * nh)))
    out = pl.pallas_call(
        _gdn_gates_kernel,
        out_shape=jax.ShapeDtypeStruct((2, nh, s), jnp.float32),
        grid=(s // bm,),
        in_specs=[
            pl.BlockSpec((bm, k), lambda i: (i, 0)),
            pl.BlockSpec((k, LANE), lambda i: (0, 0)),
            pl.BlockSpec((nh, 1), lambda i: (0, 0)),
            pl.BlockSpec((nh, 1), lambda i: (0, 0)),
        ],
        out_specs=pl.BlockSpec((2, nh, bm), lambda i: (0, 0, i)),
        compiler_params=_params("parallel"),
        name="gdn_gates",
    )(h, w_pad, a_log.reshape(nh, 1), dt_bias.reshape(nh, 1))
    return out.reshape(2, nh, 1, s)


def _gdn_core_kernel(q_ref, k_ref, v_ref, z_ref, cwq_ref, cwk_ref, cwv_ref, gate_ref, gate_prev_ref, onorm_ref, o_ref,
                     state_ref, extq_ref, extk_ref, extv_ref, kq_ref, bo_ref):
    tb = q_ref.shape[0]
    hp = q_ref.shape[1] // HEAD_DIM
    c = GDN_CHUNK
    nt = tb // c
    step = pl.program_id(1)
    slot_w = step % 2
    slot_r = 1 - slot_w

    @pl.when(step == 0)
    def _():
        state_ref[...] = jnp.zeros_like(state_ref)
        for ext in (extq_ref, extk_ref, extv_ref):
            ext[0:SUBLANE, :] = jnp.zeros((SUBLANE, hp * HEAD_DIM), jnp.float32)
        kq_ref[1] = jnp.zeros(kq_ref.shape[1:], kq_ref.dtype)
        bo_ref[1] = jnp.zeros(bo_ref.shape[1:], bo_ref.dtype)

    tiles = [(hh, n) for n in range(nt) for hh in range(hp)]
    state = [state_ref[hh] for hh in range(hp)]
    for n in range(nt):
        sl = slice(n * c, (n + 1) * c)
        for hh in range(hp):
            cols = slice(hh * HEAD_DIM, (hh + 1) * HEAD_DIM)
            ss = _dot(kq_ref[slot_r, hh, n], _bf(state[hh]))
            o = ss[c:] + bo_ref[slot_r, hh, n, c:2 * c, :]
            g_last = jnp.exp(gate_prev_ref[1, hh, :, (n + 1) * c - 1:(n + 1) * c])
            state[hh] = state[hh] * g_last - ss[:c] + bo_ref[slot_r, hh, n, 0:c, :]
            o = o * lax.rsqrt(jnp.mean(o * o, axis=-1, keepdims=True) + EPS) * onorm_ref[...]
            z = z_ref[sl, cols]
            o_ref[sl, cols] = (o * (z * jax.nn.sigmoid(z))).astype(o_ref.dtype)
    for hh in range(hp):
        state_ref[hh] = state[hh]

    def conv_silu(x_ref, ext, cw_ref):
        ext[SUBLANE:SUBLANE + tb, :] = x_ref[...]
        acc = ext[pl.ds(SUBLANE, tb), :] * cw_ref[GDN_CONV - 1:GDN_CONV, :]
        for back in range(1, GDN_CONV):
            acc = acc + ext[pl.ds(SUBLANE - back, tb), :] * cw_ref[GDN_CONV - 1 - back:GDN_CONV - back, :]
        ext[0:SUBLANE, :] = ext[tb:tb + SUBLANE, :]
        return acc * jax.nn.sigmoid(acc)

    def l2norm(x):
        return x * lax.rsqrt(jnp.sum(x * x, axis=-1, keepdims=True) + EPS)

    def head_cols(x, hh):
        return x[:, hh * HEAD_DIM:(hh + 1) * HEAD_DIM]

    q_conv = conv_silu(q_ref, extq_ref, cwq_ref)
    k_conv = conv_silu(k_ref, extk_ref, cwk_ref)
    v_conv = conv_silu(v_ref, extv_ref, cwv_ref)
    q_all = [l2norm(head_cols(q_conv, hh)) * (HEAD_DIM ** -0.5) for hh in range(hp)]
    k_all = [l2norm(head_cols(k_conv, hh)) for hh in range(hp)]
    v_all = [head_cols(v_conv, hh) for hh in range(hp)]

    row = lax.broadcasted_iota(jnp.int32, (c, c), 0)
    col = lax.broadcasted_iota(jnp.int32, (c, c), 1)
    causal = row >= col
    strict = row > col
    eye = jnp.where(row == col, 1.0, 0.0)
    pair_masks = [((row >> (l + 1)) == (col >> (l + 1))) & ((row >> l) != (col >> l)) for l in range(c.bit_length() - 1)]

    q_t, k_t, v_t, kb_t, decay_t, egc_t, beta_t, kdec_t, a_t = {}, {}, {}, {}, {}, {}, {}, {}, {}
    for t in tiles:
        hh, n = t
        sl = slice(n * c, (n + 1) * c)
        q_t[t], k_t[t], v_t[t] = q_all[hh][sl], k_all[hh][sl], v_all[hh][sl]
        gc_row = gate_ref[1, hh, :, sl]
        gc_r = jnp.broadcast_to(gc_row, (c, c))
        gc_c = gc_r.T
        beta_t[t] = jnp.broadcast_to(gate_ref[0, hh, :, sl], (c, c)).T
        decay_t[t] = jnp.where(causal, jnp.exp(jnp.where(causal, gc_c - gc_r, 0.0)), 0.0)
        egc_t[t] = jnp.exp(gc_c)
        kdec_t[t] = k_t[t] * jnp.exp(gc_row[:, c - 1:c] - gc_c)
        kb_t[t] = k_t[t] * beta_t[t]
        a_t[t] = jnp.where(strict, _dot_nt(_bf(kb_t[t]), _bf(k_t[t])) * decay_t[t], 0.0)
    t_t = {t: eye - jnp.where(pair_masks[0], a_t[t], 0.0) for t in tiles}
    a16 = {t: _bf(a_t[t]) for t in tiles}
    for off_mask in pair_masks[1:]:
        t16 = {t: _bf(t_t[t]) for t in tiles}
        ta = {t: _dot(t16[t], a16[t]) for t in tiles}
        t_t = {t: jnp.where(off_mask, t_t[t] - _dot(_bf(ta[t]), t16[t]), t_t[t]) for t in tiles}
    wu16 = {t: _bf(_dot(_bf(t_t[t]), _bf(jnp.concatenate([kb_t[t] * egc_t[t], v_t[t] * beta_t[t]], axis=1))))
            for t in tiles}
    attn16 = {t: _bf(jnp.where(causal, _dot_nt(_bf(q_t[t]), _bf(k_t[t])) * decay_t[t], 0.0)) for t in tiles}
    kw_ku = {t: _dot(_bf(kdec_t[t].T), wu16[t]) for t in tiles}
    aw_au = {t: _dot(attn16[t], wu16[t]) for t in tiles}
    for t in tiles:
        hh, n = t
        kq_ref[slot_w, hh, n, 0:c, :] = _bf(kw_ku[t][:, :HEAD_DIM])
        kq_ref[slot_w, hh, n, c:2 * c, :] = _bf(q_t[t] * egc_t[t] - aw_au[t][:, :HEAD_DIM])
        bo_ref[slot_w, hh, n, 0:c, :] = kw_ku[t][:, HEAD_DIM:]
        bo_ref[slot_w, hh, n, c:2 * c, :] = aw_au[t][:, HEAD_DIM:]


def _gdn_core(proj, conv_w, gates, out_norm, heads):
    s = proj.shape[0]
    tb = min(GDN_TB, s)
    d = HEAD_DIM
    hp = GDN_HEADS_PER_STEP
    assert heads % hp == 0
    n_hb = heads // hp
    n_tb = s // tb
    cur = lambda t: jnp.minimum(t, n_tb - 1)
    prev = lambda t: jnp.maximum(t - 1, 0)
    col = lambda off: (lambda h, t: (cur(t), off * n_hb + h))
    cw = lambda off: (lambda h, t: (0, off * n_hb + h))
    nt = tb // GDN_CHUNK
    return pl.pallas_call(
        _gdn_core_kernel,
        out_shape=jax.ShapeDtypeStruct((s, heads * d), jnp.bfloat16),
        grid=(n_hb, n_tb + 1),
        in_specs=[
            pl.BlockSpec((tb, hp * d), col(0)), pl.BlockSpec((tb, hp * d), col(1)), pl.BlockSpec((tb, hp * d), col(2)),
            pl.BlockSpec((tb, hp * d), lambda h, t: (prev(t), 3 * n_hb + h)),
            pl.BlockSpec((GDN_CONV, hp * d), cw(0)), pl.BlockSpec((GDN_CONV, hp * d), cw(1)),
            pl.BlockSpec((GDN_CONV, hp * d), cw(2)),
            pl.BlockSpec((2, hp, 1, tb), lambda h, t: (0, h, 0, cur(t))),
            pl.BlockSpec((2, hp, 1, tb), lambda h, t: (0, h, 0, prev(t))),
            pl.BlockSpec((1, d), lambda h, t: (0, 0)),
        ],
        out_specs=pl.BlockSpec((tb, hp * d), lambda h, t: (prev(t), h)),
        scratch_shapes=[pltpu.VMEM((hp, d, d), jnp.float32)] + [pltpu.VMEM((tb + SUBLANE, hp * d), jnp.float32)] * 3
        + [pltpu.VMEM((2, hp, nt, 2 * GDN_CHUNK, d), jnp.bfloat16), pltpu.VMEM((2, hp, nt, 2 * GDN_CHUNK, d), jnp.float32)],
        compiler_params=_params("parallel", "arbitrary"),
        name="gdn_core",
    )(proj, proj, proj, proj, conv_w, conv_w, conv_w, gates, gates, out_norm.reshape(1, d))


def _rmsnorm_rm_kernel(x_ref, w_ref, o_ref):
    x = x_ref[...]
    y = _bf(x * lax.rsqrt(jnp.mean(x * x, axis=-1, keepdims=True) + EPS) * w_ref[...])
    n = x.shape[0]
    bi = n // ATT_RES
    row = lax.broadcasted_iota(jnp.int32, (n, n), 0)
    col = lax.broadcasted_iota(jnp.int32, (n, n), 1)
    perm = _bf(jnp.where(col == ATT_RES * (row % bi) + row // bi, 1.0, 0.0))
    out = _dot(perm, y)
    for r in range(ATT_RES):
        o_ref[r] = out[r * bi:(r + 1) * bi].astype(o_ref.dtype)


def _rmsnorm_residue_major(x, w):
    s, d = x.shape
    n_sb = s // ATT_SB
    bi = ATT_NORM_BI
    steps = DIL_BLOCK // bi
    out = pl.pallas_call(
        _rmsnorm_rm_kernel,
        out_shape=jax.ShapeDtypeStruct((n_sb, ATT_RES, DIL_BLOCK, d), jnp.bfloat16),
        grid=(n_sb, steps),
        in_specs=[pl.BlockSpec((bi * ATT_RES, d), lambda n, i: (n * steps + i, 0)),
                  pl.BlockSpec((1, d), lambda n, i: (0, 0))],
        out_specs=pl.BlockSpec((None, ATT_RES, bi, d), lambda n, i: (n, 0, i, 0)),
        compiler_params=_params("parallel", "parallel"),
        name="rmsnorm_residue_major",
    )(x, w.reshape(1, d))
    return out.reshape(s, d)


def _mm_res_natural_kernel(a_ref, w_ref, r_ref, o_ref, o16_ref, anat_ref):
    @pl.when(pl.program_id(1) == 0)
    def _():
        bi = ATT_NORM_BI
        n = bi * ATT_RES
        row = lax.broadcasted_iota(jnp.int32, (n, n), 0)
        col = lax.broadcasted_iota(jnp.int32, (n, n), 1)
        perm = _bf(jnp.where(col == bi * (row % ATT_RES) + row // ATT_RES, 1.0, 0.0))
        for grp in range(DIL_BLOCK // bi):
            gathered = jnp.concatenate(
                [a_ref[r * DIL_BLOCK + grp * bi:r * DIL_BLOCK + (grp + 1) * bi, :] for r in range(ATT_RES)], axis=0)
            anat_ref[grp * n:(grp + 1) * n, :] = _dot(perm, gathered).astype(anat_ref.dtype)

    x = r_ref[...] + _dot(anat_ref[...], w_ref[...])
    o_ref[...] = x
    o16_ref[...] = x.astype(o16_ref.dtype)


def _matmul_res_from_residue_major(a, w, residual, name):
    m, k = a.shape
    n = w.shape[1]
    bn = min(ATT_OUT_BN, n)
    return pl.pallas_call(
        _mm_res_natural_kernel,
        out_shape=(jax.ShapeDtypeStruct((m, n), jnp.float32), jax.ShapeDtypeStruct((m, n), jnp.bfloat16)),
        grid=(m // ATT_SB, n // bn),
        in_specs=[pl.BlockSpec((ATT_SB, k), lambda i, j: (i, 0)), pl.BlockSpec((k, bn), lambda i, j: (0, j)),
                  pl.BlockSpec((ATT_SB, bn), lambda i, j: (i, j))],
        out_specs=(pl.BlockSpec((ATT_SB, bn), lambda i, j: (i, j)), pl.BlockSpec((ATT_SB, bn), lambda i, j: (i, j))),
        scratch_shapes=[pltpu.VMEM((ATT_SB, k), a.dtype)],
        compiler_params=_params("parallel", "arbitrary"),
        name=name,
    )(a, w, residual)


def _dil_attn_kernel(*refs, groups, heads_per_step):
    ng = len(groups)
    hs = heads_per_step
    q_refs = refs[:ng]
    kv_refs = refs[ng:5 * ng]
    o_ref = refs[5 * ng]
    og_refs = refs[5 * ng + 1:6 * ng + 1]
    lse_refs = refs[6 * ng + 1:7 * ng + 1]
    blk = DIL_BLOCK
    not_first = pl.program_id(0) > 0
    row = lax.broadcasted_iota(jnp.int32, (hs * blk, 2 * blk), 0) % blk
    col = lax.broadcasted_iota(jnp.int32, (hs * blk, 2 * blk), 1)
    is_prev = col < blk
    col = col % blk
    ones16 = jnp.ones((2 * blk, HEAD_DIM), jnp.bfloat16)
    neg_inf = jnp.float32(-jnp.inf)

    for gi, (window, dil) in enumerate(groups):
        span = window // dil
        per = ATT_RES // dil
        chunk = blk // per
        jq = per * (row % chunk) + row // chunk
        jk = per * (col % chunk) + col // chunk
        own_ok = (~is_prev) & (jq - jk >= 0) & (jq - jk <= span)
        prev_ok = is_prev & (jq + blk - jk <= span)
        mask = own_ok | prev_ok
        mask_first = own_ok | (prev_ok & not_first)
        k_own, k_prev, v_own, v_prev = kv_refs[4 * gi:4 * gi + 4]

        def tile_rows(ref, r_d, lo, cols=slice(None)):
            return jnp.concatenate([ref[r_d + dil * m, lo:lo + chunk, cols] for m in range(per)], axis=0)

        tiles = [(r_d, b) for r_d in range(dil) for b in range(per)]
        for t0 in range(0, len(tiles), ATT_TILE_BATCH):
            batch = tiles[t0:t0 + ATT_TILE_BATCH]
            q16, k16, v16 = {}, {}, {}
            for t in batch:
                r_d, b = t
                lo = chunk * b
                q16[t] = _bf(jnp.concatenate(
                    [tile_rows(q_refs[gi], r_d, lo, slice(h * HEAD_DIM, (h + 1) * HEAD_DIM)) for h in range(hs)], axis=0))
                if b > 0:
                    kp, vp = tile_rows(k_own, r_d, lo - chunk), tile_rows(v_own, r_d, lo - chunk)
                else:
                    kp, vp = tile_rows(k_prev, r_d, 0), tile_rows(v_prev, r_d, 0)
                k16[t] = _bf(jnp.concatenate([kp, tile_rows(k_own, r_d, lo)], axis=0))
                v16[t] = jnp.concatenate([_bf(jnp.concatenate([vp, tile_rows(v_own, r_d, lo)], axis=0)), ones16], axis=1)
            s = {t: jnp.where(mask if t[1] > 0 else mask_first, _dot_nt(q16[t], k16[t]), neg_inf) for t in batch}
            mx = {t: jnp.max(jnp.maximum(s[t][:, :blk], s[t][:, blk:]), axis=-1, keepdims=True) for t in batch}
            e16 = {t: _bf(jnp.exp(s[t] - mx[t])) for t in batch}
            pv = {t: _dot(e16[t], v16[t]) for t in batch}
            for t in batch:
                r_d, b = t
                den = pv[t][:, HEAD_DIM:]
                o_t = pv[t][:, :HEAD_DIM] / den
                lse_t = mx[t] + jnp.log(den)
                for h in range(hs):
                    for m in range(per):
                        src = slice(h * blk + m * chunk, h * blk + (m + 1) * chunk)
                        og_refs[gi][h, r_d + dil * m, chunk * b:chunk * (b + 1), :] = o_t[src]
                        lse_refs[gi][h, r_d + dil * m, chunk * b:chunk * (b + 1), :] = lse_t[src]

    for h in range(hs):
        lse_max = lse_refs[0][h]
        for gi in range(1, ng):
            lse_max = jnp.maximum(lse_max, lse_refs[gi][h])
        num = jnp.zeros(lse_max.shape, jnp.float32)
        den = jnp.zeros(lse_max.shape, jnp.float32)
        for gi in range(ng):
            wgt = jnp.exp(lse_refs[gi][h] - lse_max)
            num = num + wgt * og_refs[gi][h]
            den = den + wgt
        o_ref[:, :, h * HEAD_DIM:(h + 1) * HEAD_DIM] = (num / den).astype(o_ref.dtype)


def _dilated_attention(q, kv, heads, kv_heads):
    s = q.shape[0]
    ng = len(DIL_GROUPS)
    res, blk, d, hs = ATT_RES, DIL_BLOCK, HEAD_DIM, ATT_HEADS_PER_STEP
    assert s % ATT_SB == 0 and all(w // dil <= blk and res % dil == 0 for w, dil in DIL_GROUPS)
    n_sb = s // ATT_SB
    rep = heads // kv_heads
    assert rep % hs == 0
    q4 = q.reshape(n_sb, res, blk, q.shape[1])
    kv4 = kv.reshape(n_sb, res, blk, kv.shape[1])
    in_specs = [pl.BlockSpec((None, res, blk, hs * d),
                             lambda n, g, r, gi=gi: (n, 0, 0, (gi * heads + g * rep) // hs + r)) for gi in range(ng)]
    args = [q4] * ng
    for gi, (_, dil) in enumerate(DIL_GROUPS):
        chunk = blk // (res // dil)
        for part in range(2):
            cb = (gi * 2 + part) * kv_heads
            in_specs.append(pl.BlockSpec((None, res, blk, d), lambda n, g, r, cb=cb: (n, 0, 0, cb + g)))
            in_specs.append(pl.BlockSpec((None, res, chunk, d),
                                         lambda n, g, r, cb=cb, last=blk // chunk - 1: (jnp.maximum(n - 1, 0), 0, last, cb + g)))
            args += [kv4, kv4]
    out = pl.pallas_call(
        functools.partial(_dil_attn_kernel, groups=DIL_GROUPS, heads_per_step=hs),
        out_shape=jax.ShapeDtypeStruct((n_sb, res, blk, heads * d), jnp.bfloat16),
        grid=(n_sb, kv_heads, rep // hs),
        in_specs=in_specs,
        out_specs=pl.BlockSpec((None, res, blk, hs * d), lambda n, g, r: (n, 0, 0, (g * rep) // hs + r)),
        scratch_shapes=[pltpu.VMEM((hs, res, blk, d), jnp.float32)] * (2 * ng),
        compiler_params=_params("parallel", "parallel", "arbitrary"),
        name="dilated_attention",
    )(*args)
    return out.reshape(s, heads * d)


def _gdn_layer(x, attn_norm, w_in, conv_w, a_log, dt_bias, out_norm, w_out):
    heads = a_log.shape[0]
    qk = heads * HEAD_DIM
    h = _rmsnorm(x, attn_norm)
    proj = _matmul(h, _bf(w_in[:, :4 * qk]), name="gdn_in_proj")
    gates = _gdn_gates(h, w_in[:, 4 * qk:], a_log, dt_bias)
    o = _gdn_core(proj, conv_w, gates, out_norm, heads)
    return _matmul(o, _bf(w_out), residual=x, name="gdn_out_proj", bn=GDN_OUT_BN, bf16_copy=True)


def _shared_kv(x, kv_norm, w_kv, k_norm):
    n_dil = k_norm.shape[0]
    kvw = w_kv.shape[1] // (2 * n_dil)
    h = _rmsnorm_residue_major(x, kv_norm)
    gain = jnp.concatenate([jnp.tile(k_norm, (1, kvw // HEAD_DIM)), jnp.ones((n_dil, kvw), jnp.float32)], axis=1).reshape(1, -1)
    return _matmul_headnorm(h, _bf(w_kv), gain, scale=1.0, bn=kvw, norm_every=2, name="kv_proj")


def _dilated_layer(x, kv, attn_norm, w_q, q_norm, w_out):
    n_dil = q_norm.shape[0]
    qw = w_q.shape[1] // n_dil
    kvw = kv.shape[1] // (2 * n_dil)
    heads, kv_heads = qw // HEAD_DIM, kvw // HEAD_DIM
    h = _rmsnorm_residue_major(x, attn_norm)
    gain = jnp.tile(q_norm, (1, heads)).reshape(1, -1)
    q = _matmul_headnorm(h, _bf(w_q), gain, scale=HEAD_DIM ** -0.5, bn=min(MM_BN, qw), norm_every=1, name="q_proj")
    o = _dilated_attention(q, kv, heads, kv_heads)
    return _matmul_res_from_residue_major(o, _bf(w_out), x, name="attn_out_proj")


def _ffn_layer(x, x16, norm_w, w_gate_up, w_down):
    hidden = w_down.shape[0]
    hidden_padded = -(-hidden // MM_BN) * MM_BN
    gain = norm_w[:, None]
    act = _swiglu_up(x16, _bf(w_gate_up[:, :hidden] * gain), _bf(w_gate_up[:, hidden:] * gain), hidden_padded)
    w_down_p = jnp.pad(_bf(w_down), ((0, hidden_padded - hidden), (0, 0)))
    return _matmul(act, w_down_p, residual=x, name="ffn_down", bm=FFN_DOWN_BM, bn=FFN_DOWN_BN)


def kernel(x, a_attn_norm, a_w_in, a_conv_w, a_a_log, a_dt_bias, a_out_norm, a_w_out, kv_norm, w_kv, k_norm,
           b_attn_norm, b_w_q, b_q_norm, b_w_out, ffn_norm, ffn_w_gate_up, ffn_w_down):
    batch, s, d = x.shape
    assert batch == 1
    x = x.reshape(s, d)
    depth = ffn_norm.shape[0]
    n_a = a_attn_norm.shape[0]
    kv = None
    for layer in range(depth):
        if layer < n_a:
            i = layer
            x, x16 = _gdn_layer(x, a_attn_norm[i], a_w_in[i], a_conv_w[i], a_a_log[i], a_dt_bias[i], a_out_norm[i], a_w_out[i])
        else:
            if layer == n_a:
                kv = _shared_kv(x, kv_norm, w_kv, k_norm)
            j = layer - n_a
            x, x16 = _dilated_layer(x, kv, b_attn_norm[j], b_w_q[j], b_q_norm[j], b_w_out[j])
        x = _ffn_layer(x, x16, ffn_norm[layer], ffn_w_gate_up[layer], ffn_w_down[layer])
    return x.reshape(batch, s, d)
```

```python
import functools

import jax
import jax.numpy as jnp
from jax import lax
from jax.experimental import pallas as pl
from jax.experimental.pallas import tpu as pltpu

V7X_VMEM_LIMIT_BYTES = 56 * 1024 * 1024
LANE = 128
SUBLANE = 8

EPS = 1e-6
GDN_CONV = 4
DIL_GROUPS = ((128, 1), (512, 4), (2048, 16))
DIL_BLOCK = 128
HEAD_DIM = 128

MM_BM = 1024
MM_BN = 1024
SWIGLU_BN = 512
DOWN_BK_MAX = 3072
DOWN_BK_STEP = 256
NORM_BM = 256

GDN_CHUNK = 128
GDN_TB = 512
GDN_HEADS_PER_STEP = 4
GATES_BM = 512
ATT_RES = max(d for _, d in DIL_GROUPS)
ATT_SB = DIL_BLOCK * ATT_RES
ATT_HEADS_PER_STEP = 2
ATT_TILE_BATCH = 4
ATT_NORM_BI = 16
ATT_OUT_BN = 512


def _pick_down_bk(k_padded):
    best = DOWN_BK_STEP
    for bk in range(DOWN_BK_STEP, DOWN_BK_MAX + 1, DOWN_BK_STEP):
        if k_padded % bk == 0:
            best = bk
    return best


def _params(*semantics):
    return pltpu.CompilerParams(dimension_semantics=semantics, vmem_limit_bytes=V7X_VMEM_LIMIT_BYTES)


def _dot(a, b):
    return jnp.dot(a, b, preferred_element_type=jnp.float32)


def _dot_nt(a, b):
    return lax.dot_general(a, b, (((1,), (1,)), ((), ())), preferred_element_type=jnp.float32)


def _bf(x):
    return x.astype(jnp.bfloat16)


def _rmsnorm_kernel(x_ref, w_ref, o_ref):
    x = x_ref[...]
    y = x * lax.rsqrt(jnp.mean(x * x, axis=-1, keepdims=True) + EPS)
    o_ref[...] = (y * w_ref[...]).astype(o_ref.dtype)


def _rmsnorm(x, w, out_dtype=jnp.bfloat16):
    s, d = x.shape
    bm = min(NORM_BM, s)
    return pl.pallas_call(
        _rmsnorm_kernel,
        out_shape=jax.ShapeDtypeStruct((s, d), out_dtype),
        grid=(s // bm,),
        in_specs=[pl.BlockSpec((bm, d), lambda i: (i, 0)), pl.BlockSpec((1, d), lambda i: (0, 0))],
        out_specs=pl.BlockSpec((bm, d), lambda i: (i, 0)),
        compiler_params=_params("parallel"),
        name="rmsnorm",
    )(x, w.reshape(1, d))


def _mm_kernel(a_ref, w_ref, o_ref):
    o_ref[...] = _dot(a_ref[...], w_ref[...]).astype(o_ref.dtype)


def _mm_res_kernel(a_ref, w_ref, r_ref, o_ref):
    o_ref[...] = (r_ref[...] + _dot(a_ref[...], w_ref[...])).astype(o_ref.dtype)


def _matmul(a, w, residual=None, out_dtype=jnp.float32, name="matmul"):
    m, k = a.shape
    n = w.shape[1]
    bm, bn = min(MM_BM, m), min(MM_BN, n)
    in_specs = [pl.BlockSpec((bm, k), lambda i, j: (i, 0)), pl.BlockSpec((k, bn), lambda i, j: (0, j))]
    args = [a, w]
    kern = _mm_kernel
    if residual is not None:
        in_specs.append(pl.BlockSpec((bm, bn), lambda i, j: (i, j)))
        args.append(residual)
        kern = _mm_res_kernel
    return pl.pallas_call(
        kern,
        out_shape=jax.ShapeDtypeStruct((m, n), out_dtype),
        grid=(m // bm, n // bn),
        in_specs=in_specs,
        out_specs=pl.BlockSpec((bm, bn), lambda i, j: (i, j)),
        compiler_params=_params("parallel", "parallel"),
        name=name,
    )(*args)


def _mm_headnorm_kernel(a_ref, w_ref, g_ref, o_ref, *, scale, norm_every):
    y = _dot(a_ref[...], w_ref[...])
    bn = y.shape[1]

    def normed():
        for c in range(bn // LANE):
            ys = y[:, c * LANE:(c + 1) * LANE]
            inv = lax.rsqrt(jnp.mean(ys * ys, axis=-1, keepdims=True) + EPS)
            o_ref[:, c * LANE:(c + 1) * LANE] = (ys * inv * g_ref[:, c * LANE:(c + 1) * LANE] * scale).astype(o_ref.dtype)

    if norm_every == 1:
        normed()
    else:
        j = pl.program_id(1)

        @pl.when(j % norm_every == 0)
        def _():
            normed()

        @pl.when(j % norm_every != 0)
        def _():
            o_ref[...] = y.astype(o_ref.dtype)


def _matmul_headnorm(a, w, gain_row, *, scale, bn, norm_every, name):
    m, k = a.shape
    n = w.shape[1]
    bm = min(MM_BM, m)
    return pl.pallas_call(
        functools.partial(_mm_headnorm_kernel, scale=scale, norm_every=norm_every),
        out_shape=jax.ShapeDtypeStruct((m, n), jnp.float32),
        grid=(m // bm, n // bn),
        in_specs=[
            pl.BlockSpec((bm, k), lambda i, j: (i, 0)),
            pl.BlockSpec((k, bn), lambda i, j: (0, j)),
            pl.BlockSpec((1, bn), lambda i, j: (0, j)),
        ],
        out_specs=pl.BlockSpec((bm, bn), lambda i, j: (i, j)),
        compiler_params=_params("parallel", "parallel"),
        name=name,
    )(a, w, gain_row)


def _swiglu_up_kernel(h_ref, wg_ref, wu_ref, o_ref, *, hidden):
    h = h_ref[...]
    g = _dot(h, wg_ref[...])
    u = _dot(h, wu_ref[...])
    act = g * jax.nn.sigmoid(g) * u
    bn = act.shape[1]
    col = pl.program_id(1) * bn + lax.broadcasted_iota(jnp.int32, act.shape, 1)
    o_ref[...] = jnp.where(col < hidden, act, 0.0).astype(o_ref.dtype)


def _swiglu_up(h, wg, wu, hidden_padded):
    m, k = h.shape
    hidden = wg.shape[1]
    bm, bn = min(MM_BM, m), SWIGLU_BN
    return pl.pallas_call(
        functools.partial(_swiglu_up_kernel, hidden=hidden),
        out_shape=jax.ShapeDtypeStruct((m, hidden_padded), jnp.bfloat16),
        grid=(m // bm, hidden_padded // bn),
        in_specs=[
            pl.BlockSpec((bm, k), lambda i, j: (i, 0)),
            pl.BlockSpec((k, bn), lambda i, j: (0, j)),
            pl.BlockSpec((k, bn), lambda i, j: (0, j)),
        ],
        out_specs=pl.BlockSpec((bm, bn), lambda i, j: (i, j)),
        compiler_params=_params("parallel", "parallel"),
        name="swiglu_up",
    )(h, wg, wu)


def _mm_acc_res_kernel(a_ref, w_ref, r_ref, o_ref, acc_ref):
    kk = pl.program_id(2)

    @pl.when(kk == 0)
    def _():
        acc_ref[...] = r_ref[...]

    acc_ref[...] += _dot(a_ref[...], w_ref[...])

    @pl.when(kk == pl.num_programs(2) - 1)
    def _():
        o_ref[...] = acc_ref[...]


def _matmul_ktiled_res(a, w, residual, bk, name):
    m, k = a.shape
    n = w.shape[1]
    bm, bn = min(MM_BM, m), min(MM_BN, n)
    return pl.pallas_call(
        _mm_acc_res_kernel,
        out_shape=jax.ShapeDtypeStruct((m, n), jnp.float32),
        grid=(m // bm, n // bn, k // bk),
        in_specs=[
            pl.BlockSpec((bm, bk), lambda i, j, kk: (i, kk)),
            pl.BlockSpec((bk, bn), lambda i, j, kk: (kk, j)),
            pl.BlockSpec((bm, bn), lambda i, j, kk: (i, j)),
        ],
        out_specs=pl.BlockSpec((bm, bn), lambda i, j, kk: (i, j)),
        scratch_shapes=[pltpu.VMEM((bm, bn), jnp.float32)],
        compiler_params=_params("parallel", "parallel", "arbitrary"),
        name=name,
    )(a, w, residual)


def _gdn_gates_kernel(h_ref, w_ref, alog_ref, dt_ref, o_ref):
    nh = alog_ref.shape[0]
    yt = _dot(h_ref[...], w_ref[...]).T
    o_ref[0] = jax.nn.sigmoid(yt[:nh])
    x = yt[nh:2 * nh] + dt_ref[...]
    softplus = jnp.maximum(x, 0.0) + jnp.log1p(jnp.exp(-jnp.abs(x)))
    g = -jnp.exp(alog_ref[...]) * softplus
    lane = lax.broadcasted_iota(jnp.int32, (nh, GDN_CHUNK), 1)
    for c in range(g.shape[1] // GDN_CHUNK):
        acc = g[:, c * GDN_CHUNK:(c + 1) * GDN_CHUNK]
        shift = 1
        while shift < GDN_CHUNK:
            acc = acc + jnp.where(lane >= shift, pltpu.roll(acc, shift, 1), 0.0)
            shift *= 2
        o_ref[1, :, c * GDN_CHUNK:(c + 1) * GDN_CHUNK] = acc


def _gdn_gates(h, w_gate, a_log, dt_bias):
    s, k = h.shape
    nh = a_log.shape[0]
    bm = min(GATES_BM, s)
    w_pad = jnp.pad(_bf(w_gate), ((0, 0), (0, LANE - 2 * nh)))
    out = pl.pallas_call(
        _gdn_gates_kernel,
        out_shape=jax.ShapeDtypeStruct((2, nh, s), jnp.float32),
        grid=(s // bm,),
        in_specs=[
            pl.BlockSpec((bm, k), lambda i: (i, 0)),
            pl.BlockSpec((k, LANE), lambda i: (0, 0)),
            pl.BlockSpec((nh, 1), lambda i: (0, 0)),
            pl.BlockSpec((nh, 1), lambda i: (0, 0)),
        ],
        out_specs=pl.BlockSpec((2, nh, bm), lambda i: (0, 0, i)),
        compiler_params=_params("parallel"),
        name="gdn_gates",
    )(h, w_pad, a_log.reshape(nh, 1), dt_bias.reshape(nh, 1))
    return out.reshape(2, nh, 1, s)


def _gdn_core_kernel(q_ref, k_ref, v_ref, z_ref, cwq_ref, cwk_ref, cwv_ref, gate_ref, gate_prev_ref, onorm_ref, o_ref,
                     state_ref, extq_ref, extk_ref, extv_ref, kq_ref, bo_ref):
    tb = q_ref.shape[0]
    hp = q_ref.shape[1] // HEAD_DIM
    c = GDN_CHUNK
    nt = tb // c
    step = pl.program_id(1)
    slot_w = step % 2
    slot_r = 1 - slot_w

    @pl.when(step == 0)
    def _():
        state_ref[...] = jnp.zeros_like(state_ref)
        for ext in (extq_ref, extk_ref, extv_ref):
            ext[0:SUBLANE, :] = jnp.zeros((SUBLANE, hp * HEAD_DIM), jnp.float32)
        kq_ref[1] = jnp.zeros(kq_ref.shape[1:], kq_ref.dtype)
        bo_ref[1] = jnp.zeros(bo_ref.shape[1:], bo_ref.dtype)

    tiles = [(hh, n) for n in range(nt) for hh in range(hp)]
    state = [state_ref[hh] for hh in range(hp)]
    for n in range(nt):
        sl = slice(n * c, (n + 1) * c)
        for hh in range(hp):
            cols = slice(hh * HEAD_DIM, (hh + 1) * HEAD_DIM)
            ss = _dot(kq_ref[slot_r, hh, n], _bf(state[hh]))
            o = ss[c:] + bo_ref[slot_r, hh, n, c:2 * c, :]
            g_last = jnp.exp(gate_prev_ref[1, hh, :, (n + 1) * c - 1:(n + 1) * c])
            state[hh] = state[hh] * g_last - ss[:c] + bo_ref[slot_r, hh, n, 0:c, :]
            o = o * lax.rsqrt(jnp.mean(o * o, axis=-1, keepdims=True) + EPS) * onorm_ref[...]
            z = z_ref[sl, cols]
            o_ref[sl, cols] = (o * (z * jax.nn.sigmoid(z))).astype(o_ref.dtype)
    for hh in range(hp):
        state_ref[hh] = state[hh]

    def conv_silu(x_ref, ext, cw_ref):
        ext[SUBLANE:SUBLANE + tb, :] = x_ref[...]
        acc = ext[pl.ds(SUBLANE, tb), :] * cw_ref[GDN_CONV - 1:GDN_CONV, :]
        for back in range(1, GDN_CONV):
            acc = acc + ext[pl.ds(SUBLANE - back, tb), :] * cw_ref[GDN_CONV - 1 - back:GDN_CONV - back, :]
        ext[0:SUBLANE, :] = ext[tb:tb + SUBLANE, :]
        return acc * jax.nn.sigmoid(acc)

    def l2norm(x):
        return x * lax.rsqrt(jnp.sum(x * x, axis=-1, keepdims=True) + EPS)

    def head_cols(x, hh):
        return x[:, hh * HEAD_DIM:(hh + 1) * HEAD_DIM]

    q_conv = conv_silu(q_ref, extq_ref, cwq_ref)
    k_conv = conv_silu(k_ref, extk_ref, cwk_ref)
    v_conv = conv_silu(v_ref, extv_ref, cwv_ref)
    q_all = [l2norm(head_cols(q_conv, hh)) * (HEAD_DIM ** -0.5) for hh in range(hp)]
    k_all = [l2norm(head_cols(k_conv, hh)) for hh in range(hp)]
    v_all = [head_cols(v_conv, hh) for hh in range(hp)]

    row = lax.broadcasted_iota(jnp.int32, (c, c), 0)
    col = lax.broadcasted_iota(jnp.int32, (c, c), 1)
    causal = row >= col
    strict = row > col
    eye = jnp.where(row == col, 1.0, 0.0)
    pair_masks = [((row >> (l + 1)) == (col >> (l + 1))) & ((row >> l) != (col >> l)) for l in range(c.bit_length() - 1)]

    q_t, k_t, v_t, kb_t, decay_t, egc_t, beta_t, kdec_t, a_t = {}, {}, {}, {}, {}, {}, {}, {}, {}
    for t in tiles:
        hh, n = t
        sl = slice(n * c, (n + 1) * c)
        q_t[t], k_t[t], v_t[t] = q_all[hh][sl], k_all[hh][sl], v_all[hh][sl]
        gc_row = gate_ref[1, hh, :, sl]
        gc_r = jnp.broadcast_to(gc_row, (c, c))
        gc_c = gc_r.T
        beta_t[t] = jnp.broadcast_to(gate_ref[0, hh, :, sl], (c, c)).T
        decay_t[t] = jnp.where(causal, jnp.exp(gc_c - gc_r), 0.0)
        egc_t[t] = jnp.exp(gc_c)
        kdec_t[t] = k_t[t] * jnp.exp(gc_row[:, c - 1:c] - gc_c)
        kb_t[t] = k_t[t] * beta_t[t]
        a_t[t] = jnp.where(strict, _dot_nt(_bf(kb_t[t]), _bf(k_t[t])) * decay_t[t], 0.0)
    t_t = {t: eye - jnp.where(pair_masks[0], a_t[t], 0.0) for t in tiles}
    a16 = {t: _bf(a_t[t]) for t in tiles}
    for off_mask in pair_masks[1:]:
        t16 = {t: _bf(t_t[t]) for t in tiles}
        ta = {t: _dot(t16[t], a16[t]) for t in tiles}
        t_t = {t: jnp.where(off_mask, t_t[t] - _dot(_bf(ta[t]), t16[t]), t_t[t]) for t in tiles}
    wu16 = {t: _bf(_dot(_bf(t_t[t]), _bf(jnp.concatenate([kb_t[t] * egc_t[t], v_t[t] * beta_t[t]], axis=1))))
            for t in tiles}
    attn16 = {t: _bf(jnp.where(causal, _dot_nt(_bf(q_t[t]), _bf(k_t[t])) * decay_t[t], 0.0)) for t in tiles}
    kw_ku = {t: _dot(_bf(kdec_t[t].T), wu16[t]) for t in tiles}
    aw_au = {t: _dot(attn16[t], wu16[t]) for t in tiles}
    for t in tiles:
        hh, n = t
        kq_ref[slot_w, hh, n, 0:c, :] = _bf(kw_ku[t][:, :HEAD_DIM])
        kq_ref[slot_w, hh, n, c:2 * c, :] = _bf(q_t[t] * egc_t[t] - aw_au[t][:, :HEAD_DIM])
        bo_ref[slot_w, hh, n, 0:c, :] = kw_ku[t][:, HEAD_DIM:]
        bo_ref[slot_w, hh, n, c:2 * c, :] = aw_au[t][:, HEAD_DIM:]


def _gdn_core(proj, conv_w, gates, out_norm, heads):
    s = proj.shape[0]
    tb = min(GDN_TB, s)
    d = HEAD_DIM
    hp = GDN_HEADS_PER_STEP
    assert heads % hp == 0
    n_hb = heads // hp
    n_tb = s // tb
    cur = lambda t: jnp.minimum(t, n_tb - 1)
    prev = lambda t: jnp.maximum(t - 1, 0)
    col = lambda off: (lambda h, t: (cur(t), off * n_hb + h))
    cw = lambda off: (lambda h, t: (0, off * n_hb + h))
    nt = tb // GDN_CHUNK
    return pl.pallas_call(
        _gdn_core_kernel,
        out_shape=jax.ShapeDtypeStruct((s, heads * d), jnp.bfloat16),
        grid=(n_hb, n_tb + 1),
        in_specs=[
            pl.BlockSpec((tb, hp * d), col(0)), pl.BlockSpec((tb, hp * d), col(1)), pl.BlockSpec((tb, hp * d), col(2)),
            pl.BlockSpec((tb, hp * d), lambda h, t: (prev(t), 3 * n_hb + h)),
            pl.BlockSpec((GDN_CONV, hp * d), cw(0)), pl.BlockSpec((GDN_CONV, hp * d), cw(1)),
            pl.BlockSpec((GDN_CONV, hp * d), cw(2)),
            pl.BlockSpec((2, hp, 1, tb), lambda h, t: (0, h, 0, cur(t))),
            pl.BlockSpec((2, hp, 1, tb), lambda h, t: (0, h, 0, prev(t))),
            pl.BlockSpec((1, d), lambda h, t: (0, 0)),
        ],
        out_specs=pl.BlockSpec((tb, hp * d), lambda h, t: (prev(t), h)),
        scratch_shapes=[pltpu.VMEM((hp, d, d), jnp.float32)] + [pltpu.VMEM((tb + SUBLANE, hp * d), jnp.float32)] * 3
        + [pltpu.VMEM((2, hp, nt, 2 * GDN_CHUNK, d), jnp.bfloat16), pltpu.VMEM((2, hp, nt, 2 * GDN_CHUNK, d), jnp.float32)],
        compiler_params=_params("parallel", "arbitrary"),
        name="gdn_core",
    )(proj, proj, proj, proj, conv_w, conv_w, conv_w, gates, gates, out_norm.reshape(1, d))


def _rmsnorm_rm_kernel(x_ref, w_ref, o_ref):
    x = x_ref[...]
    y = _bf(x * lax.rsqrt(jnp.mean(x * x, axis=-1, keepdims=True) + EPS) * w_ref[...])
    n = x.shape[0]
    bi = n // ATT_RES
    row = lax.broadcasted_iota(jnp.int32, (n, n), 0)
    col = lax.broadcasted_iota(jnp.int32, (n, n), 1)
    perm = _bf(jnp.where(col == ATT_RES * (row % bi) + row // bi, 1.0, 0.0))
    out = _dot(perm, y)
    for r in range(ATT_RES):
        o_ref[r] = out[r * bi:(r + 1) * bi].astype(o_ref.dtype)


def _rmsnorm_residue_major(x, w):
    s, d = x.shape
    n_sb = s // ATT_SB
    bi = ATT_NORM_BI
    steps = DIL_BLOCK // bi
    out = pl.pallas_call(
        _rmsnorm_rm_kernel,
        out_shape=jax.ShapeDtypeStruct((n_sb, ATT_RES, DIL_BLOCK, d), jnp.bfloat16),
        grid=(n_sb, steps),
        in_specs=[pl.BlockSpec((bi * ATT_RES, d), lambda n, i: (n * steps + i, 0)),
                  pl.BlockSpec((1, d), lambda n, i: (0, 0))],
        out_specs=pl.BlockSpec((None, ATT_RES, bi, d), lambda n, i: (n, 0, i, 0)),
        compiler_params=_params("parallel", "parallel"),
        name="rmsnorm_residue_major",
    )(x, w.reshape(1, d))
    return out.reshape(s, d)


def _mm_res_natural_kernel(a_ref, w_ref, r_ref, o_ref, anat_ref):
    @pl.when(pl.program_id(1) == 0)
    def _():
        bi = ATT_NORM_BI
        n = bi * ATT_RES
        row = lax.broadcasted_iota(jnp.int32, (n, n), 0)
        col = lax.broadcasted_iota(jnp.int32, (n, n), 1)
        perm = _bf(jnp.where(col == bi * (row % ATT_RES) + row // ATT_RES, 1.0, 0.0))
        for grp in range(DIL_BLOCK // bi):
            gathered = jnp.concatenate(
                [a_ref[r * DIL_BLOCK + grp * bi:r * DIL_BLOCK + (grp + 1) * bi, :] for r in range(ATT_RES)], axis=0)
            anat_ref[grp * n:(grp + 1) * n, :] = _dot(perm, gathered).astype(anat_ref.dtype)

    o_ref[...] = r_ref[...] + _dot(anat_ref[...], w_ref[...])


def _matmul_res_from_residue_major(a, w, residual, name):
    m, k = a.shape
    n = w.shape[1]
    bn = min(ATT_OUT_BN, n)
    return pl.pallas_call(
        _mm_res_natural_kernel,
        out_shape=jax.ShapeDtypeStruct((m, n), jnp.float32),
        grid=(m // ATT_SB, n // bn),
        in_specs=[pl.BlockSpec((ATT_SB, k), lambda i, j: (i, 0)), pl.BlockSpec((k, bn), lambda i, j: (0, j)),
                  pl.BlockSpec((ATT_SB, bn), lambda i, j: (i, j))],
        out_specs=pl.BlockSpec((ATT_SB, bn), lambda i, j: (i, j)),
        scratch_shapes=[pltpu.VMEM((ATT_SB, k), a.dtype)],
        compiler_params=_params("parallel", "arbitrary"),
        name=name,
    )(a, w, residual)


def _dil_attn_kernel(*refs, groups, heads_per_step):
    ng = len(groups)
    hs = heads_per_step
    q_refs = refs[:ng]
    kv_refs = refs[ng:5 * ng]
    o_ref = refs[5 * ng]
    og_refs = refs[5 * ng + 1:6 * ng + 1]
    lse_refs = refs[6 * ng + 1:7 * ng + 1]
    blk = DIL_BLOCK
    not_first = pl.program_id(0) > 0
    row = lax.broadcasted_iota(jnp.int32, (hs * blk, 2 * blk), 0) % blk
    col = lax.broadcasted_iota(jnp.int32, (hs * blk, 2 * blk), 1)
    is_prev = col < blk
    col = col % blk
    ones16 = jnp.ones((2 * blk, HEAD_DIM), jnp.bfloat16)
    neg_inf = jnp.float32(-jnp.inf)

    for gi, (window, dil) in enumerate(groups):
        span = window // dil
        per = ATT_RES // dil
        chunk = blk // per
        jq = per * (row % chunk) + row // chunk
        jk = per * (col % chunk) + col // chunk
        own_ok = (~is_prev) & (jq - jk >= 0) & (jq - jk <= span)
        prev_ok = is_prev & (jq + blk - jk <= span)
        mask = own_ok | prev_ok
        mask_first = own_ok | (prev_ok & not_first)
        k_own, k_prev, v_own, v_prev = kv_refs[4 * gi:4 * gi + 4]

        def tile_rows(ref, r_d, lo, cols=slice(None)):
            return jnp.concatenate([ref[r_d + dil * m, lo:lo + chunk, cols] for m in range(per)], axis=0)

        tiles = [(r_d, b) for r_d in range(dil) for b in range(per)]
        for t0 in range(0, len(tiles), ATT_TILE_BATCH):
            batch = tiles[t0:t0 + ATT_TILE_BATCH]
            q16, k16, v16 = {}, {}, {}
            for t in batch:
                r_d, b = t
                lo = chunk * b
                q16[t] = _bf(jnp.concatenate(
                    [tile_rows(q_refs[gi], r_d, lo, slice(h * HEAD_DIM, (h + 1) * HEAD_DIM)) for h in range(hs)], axis=0))
                if b > 0:
                    kp, vp = tile_rows(k_own, r_d, lo - chunk), tile_rows(v_own, r_d, lo - chunk)
                else:
                    kp, vp = tile_rows(k_prev, r_d, 0), tile_rows(v_prev, r_d, 0)
                k16[t] = _bf(jnp.concatenate([kp, tile_rows(k_own, r_d, lo)], axis=0))
                v16[t] = jnp.concatenate([_bf(jnp.concatenate([vp, tile_rows(v_own, r_d, lo)], axis=0)), ones16], axis=1)
            s = {t: jnp.where(mask if t[1] > 0 else mask_first, _dot_nt(q16[t], k16[t]), neg_inf) for t in batch}
            mx = {t: jnp.max(jnp.maximum(s[t][:, :blk], s[t][:, blk:]), axis=-1, keepdims=True) for t in batch}
            e16 = {t: _bf(jnp.exp(s[t] - mx[t])) for t in batch}
            pv = {t: _dot(e16[t], v16[t]) for t in batch}
            for t in batch:
                r_d, b = t
                den = pv[t][:, HEAD_DIM:]
                o_t = pv[t][:, :HEAD_DIM] / den
                lse_t = mx[t] + jnp.log(den)
                for h in range(hs):
                    for m in range(per):
                        src = slice(h * blk + m * chunk, h * blk + (m + 1) * chunk)
                        og_refs[gi][h, r_d + dil * m, chunk * b:chunk * (b + 1), :] = o_t[src]
                        lse_refs[gi][h, r_d + dil * m, chunk * b:chunk * (b + 1), :] = lse_t[src]

    for h in range(hs):
        lse_max = lse_refs[0][h]
        for gi in range(1, ng):
            lse_max = jnp.maximum(lse_max, lse_refs[gi][h])
        num = jnp.zeros(lse_max.shape, jnp.float32)
        den = jnp.zeros(lse_max.shape, jnp.float32)
        for gi in range(ng):
            wgt = jnp.exp(lse_refs[gi][h] - lse_max)
            num = num + wgt * og_refs[gi][h]
            den = den + wgt
        o_ref[:, :, h * HEAD_DIM:(h + 1) * HEAD_DIM] = (num / den).astype(o_ref.dtype)


def _dilated_attention(q, kv, heads, kv_heads):
    s = q.shape[0]
    ng = len(DIL_GROUPS)
    res, blk, d, hs = ATT_RES, DIL_BLOCK, HEAD_DIM, ATT_HEADS_PER_STEP
    assert s % ATT_SB == 0 and all(w // dil <= blk and res % dil == 0 for w, dil in DIL_GROUPS)
    n_sb = s // ATT_SB
    rep = heads // kv_heads
    assert rep % hs == 0
    q4 = q.reshape(n_sb, res, blk, q.shape[1])
    kv4 = kv.reshape(n_sb, res, blk, kv.shape[1])
    in_specs = [pl.BlockSpec((None, res, blk, hs * d),
                             lambda n, g, r, gi=gi: (n, 0, 0, (gi * heads + g * rep) // hs + r)) for gi in range(ng)]
    args = [q4] * ng
    for gi, (_, dil) in enumerate(DIL_GROUPS):
        chunk = blk // (res // dil)
        for part in range(2):
            cb = (gi * 2 + part) * kv_heads
            in_specs.append(pl.BlockSpec((None, res, blk, d), lambda n, g, r, cb=cb: (n, 0, 0, cb + g)))
            in_specs.append(pl.BlockSpec((None, res, chunk, d),
                                         lambda n, g, r, cb=cb, last=blk // chunk - 1: (jnp.maximum(n - 1, 0), 0, last, cb + g)))
            args += [kv4, kv4]
    out = pl.pallas_call(
        functools.partial(_dil_attn_kernel, groups=DIL_GROUPS, heads_per_step=hs),
        out_shape=jax.ShapeDtypeStruct((n_sb, res, blk, heads * d), jnp.bfloat16),
        grid=(n_sb, kv_heads, rep // hs),
        in_specs=in_specs,
        out_specs=pl.BlockSpec((None, res, blk, hs * d), lambda n, g, r: (n, 0, 0, (g * rep) // hs + r)),
        scratch_shapes=[pltpu.VMEM((hs, res, blk, d), jnp.float32)] * (2 * ng),
        compiler_params=_params("parallel", "parallel", "arbitrary"),
        name="dilated_attention",
    )(*args)
    return out.reshape(s, heads * d)


def _gdn_layer(x, attn_norm, w_in, conv_w, a_log, dt_bias, out_norm, w_out):
    heads = a_log.shape[0]
    qk = heads * HEAD_DIM
    h = _rmsnorm(x, attn_norm)
    proj = _matmul(h, _bf(w_in[:, :4 * qk]), name="gdn_in_proj")
    gates = _gdn_gates(h, w_in[:, 4 * qk:], a_log, dt_bias)
    o = _gdn_core(proj, conv_w, gates, out_norm, heads)
    return _matmul(o, _bf(w_out), residual=x, name="gdn_out_proj")


def _shared_kv(x, kv_norm, w_kv, k_norm):
    n_dil = k_norm.shape[0]
    kvw = w_kv.shape[1] // (2 * n_dil)
    h = _rmsnorm_residue_major(x, kv_norm)
    gain = jnp.concatenate([jnp.tile(k_norm, (1, kvw // HEAD_DIM)), jnp.ones((n_dil, kvw), jnp.float32)], axis=1).reshape(1, -1)
    return _matmul_headnorm(h, _bf(w_kv), gain, scale=1.0, bn=kvw, norm_every=2, name="kv_proj")


def _dilated_layer(x, kv, attn_norm, w_q, q_norm, w_out):
    n_dil = q_norm.shape[0]
    qw = w_q.shape[1] // n_dil
    kvw = kv.shape[1] // (2 * n_dil)
    heads, kv_heads = qw // HEAD_DIM, kvw // HEAD_DIM
    h = _rmsnorm_residue_major(x, attn_norm)
    gain = jnp.tile(q_norm, (1, heads)).reshape(1, -1)
    q = _matmul_headnorm(h, _bf(w_q), gain, scale=HEAD_DIM ** -0.5, bn=min(MM_BN, qw), norm_every=1, name="q_proj")
    o = _dilated_attention(q, kv, heads, kv_heads)
    return _matmul_res_from_residue_major(o, _bf(w_out), x, name="attn_out_proj")


def _ffn_layer(x, norm_w, w_gate_up, w_down):
    hidden = w_down.shape[0]
    hidden_padded = -(-hidden // MM_BN) * MM_BN
    h = _rmsnorm(x, norm_w)
    act = _swiglu_up(h, _bf(w_gate_up[:, :hidden]), _bf(w_gate_up[:, hidden:]), hidden_padded)
    w_down_p = _bf(jnp.pad(w_down, ((0, hidden_padded - hidden), (0, 0))))
    return _matmul_ktiled_res(act, w_down_p, x, _pick_down_bk(hidden_padded), name="ffn_down")


def kernel(x, a_attn_norm, a_w_in, a_conv_w, a_a_log, a_dt_bias, a_out_norm, a_w_out, kv_norm, w_kv, k_norm,
           b_attn_norm, b_w_q, b_q_norm, b_w_out, ffn_norm, ffn_w_gate_up, ffn_w_down):
    batch, s, d = x.shape
    assert batch == 1
    x = x.reshape(s, d)
    depth = ffn_norm.shape[0]
    n_a = a_attn_norm.shape[0]
    kv = None
    for layer in range(depth):
        if layer < n_a:
            i = layer
            x = _gdn_layer(x, a_attn_norm[i], a_w_in[i], a_conv_w[i], a_a_log[i], a_dt_bias[i], a_out_norm[i], a_w_out[i])
        else:
            if layer == n_a:
                kv = _shared_kv(x, kv_norm, w_kv, k_norm)
            j = layer - n_a
            x = _dilated_layer(x, kv, b_attn_norm[j], b_w_q[j], b_q_norm[j], b_w_out[j])
        x = _ffn_layer(x, ffn_norm[layer], ffn_w_gate_up[layer], ffn_w_down[layer])
    return x.reshape(batch, s, d)
```

```python
import functools

import jax
import jax.numpy as jnp
from jax import lax
from jax.experimental import pallas as pl
from jax.experimental.pallas import tpu as pltpu

V7X_VMEM_LIMIT_BYTES = 56 * 1024 * 1024
LANE = 128
SUBLANE = 8

EPS = 1e-6
GDN_CONV = 4
DIL_GROUPS = ((128, 1), (512, 4), (2048, 16))
DIL_BLOCK = 128
HEAD_DIM = 128

MM_BM = 1024
MM_BN = 1024
SWIGLU_BN = 512
DOWN_BK_MAX = 3072
DOWN_BK_STEP = 256
NORM_BM = 256

GDN_CHUNK = 128
GDN_TB = 512
GDN_HEADS_PER_STEP = 4
GATES_BM = 512
ATT_RES = max(d for _, d in DIL_GROUPS)
ATT_SB = DIL_BLOCK * ATT_RES
ATT_HEADS_PER_STEP = 2
ATT_TILE_BATCH = 4
ATT_NORM_BI = 16
ATT_OUT_BN = 512


def _pick_down_bk(k_padded):
    best = DOWN_BK_STEP
    for bk in range(DOWN_BK_STEP, DOWN_BK_MAX + 1, DOWN_BK_STEP):
        if k_padded % bk == 0:
            best = bk
    return best


def _params(*semantics):
    return pltpu.CompilerParams(dimension_semantics=semantics, vmem_limit_bytes=V7X_VMEM_LIMIT_BYTES)


def _dot(a, b):
    return jnp.dot(a, b, preferred_element_type=jnp.float32)


def _dot_nt(a, b):
    return lax.dot_general(a, b, (((1,), (1,)), ((), ())), preferred_element_type=jnp.float32)


def _bf(x):
    return x.astype(jnp.bfloat16)


def _rmsnorm_kernel(x_ref, w_ref, o_ref):
    x = x_ref[...]
    y = x * lax.rsqrt(jnp.mean(x * x, axis=-1, keepdims=True) + EPS)
    o_ref[...] = (y * w_ref[...]).astype(o_ref.dtype)


def _rmsnorm(x, w, out_dtype=jnp.bfloat16):
    s, d = x.shape
    bm = min(NORM_BM, s)
    return pl.pallas_call(
        _rmsnorm_kernel,
        out_shape=jax.ShapeDtypeStruct((s, d), out_dtype),
        grid=(s // bm,),
        in_specs=[pl.BlockSpec((bm, d), lambda i: (i, 0)), pl.BlockSpec((1, d), lambda i: (0, 0))],
        out_specs=pl.BlockSpec((bm, d), lambda i: (i, 0)),
        compiler_params=_params("parallel"),
        name="rmsnorm",
    )(x, w.reshape(1, d))


def _mm_kernel(a_ref, w_ref, o_ref):
    o_ref[...] = _dot(a_ref[...], w_ref[...]).astype(o_ref.dtype)


def _mm_res_kernel(a_ref, w_ref, r_ref, o_ref):
    o_ref[...] = (r_ref[...] + _dot(a_ref[...], w_ref[...])).astype(o_ref.dtype)


def _matmul(a, w, residual=None, out_dtype=jnp.float32, name="matmul"):
    m, k = a.shape
    n = w.shape[1]
    bm, bn = min(MM_BM, m), min(MM_BN, n)
    in_specs = [pl.BlockSpec((bm, k), lambda i, j: (i, 0)), pl.BlockSpec((k, bn), lambda i, j: (0, j))]
    args = [a, w]
    kern = _mm_kernel
    if residual is not None:
        in_specs.append(pl.BlockSpec((bm, bn), lambda i, j: (i, j)))
        args.append(residual)
        kern = _mm_res_kernel
    return pl.pallas_call(
        kern,
        out_shape=jax.ShapeDtypeStruct((m, n), out_dtype),
        grid=(m // bm, n // bn),
        in_specs=in_specs,
        out_specs=pl.BlockSpec((bm, bn), lambda i, j: (i, j)),
        compiler_params=_params("parallel", "parallel"),
        name=name,
    )(*args)


def _mm_headnorm_kernel(a_ref, w_ref, g_ref, o_ref, *, scale, norm_every):
    y = _dot(a_ref[...], w_ref[...])
    bn = y.shape[1]

    def normed():
        for c in range(bn // LANE):
            ys = y[:, c * LANE:(c + 1) * LANE]
            inv = lax.rsqrt(jnp.mean(ys * ys, axis=-1, keepdims=True) + EPS)
            o_ref[:, c * LANE:(c + 1) * LANE] = (ys * inv * g_ref[:, c * LANE:(c + 1) * LANE] * scale).astype(o_ref.dtype)

    if norm_every == 1:
        normed()
    else:
        j = pl.program_id(1)

        @pl.when(j % norm_every == 0)
        def _():
            normed()

        @pl.when(j % norm_every != 0)
        def _():
            o_ref[...] = y.astype(o_ref.dtype)


def _matmul_headnorm(a, w, gain_row, *, scale, bn, norm_every, name):
    m, k = a.shape
    n = w.shape[1]
    bm = min(MM_BM, m)
    return pl.pallas_call(
        functools.partial(_mm_headnorm_kernel, scale=scale, norm_every=norm_every),
        out_shape=jax.ShapeDtypeStruct((m, n), jnp.float32),
        grid=(m // bm, n // bn),
        in_specs=[
            pl.BlockSpec((bm, k), lambda i, j: (i, 0)),
            pl.BlockSpec((k, bn), lambda i, j: (0, j)),
            pl.BlockSpec((1, bn), lambda i, j: (0, j)),
        ],
        out_specs=pl.BlockSpec((bm, bn), lambda i, j: (i, j)),
        compiler_params=_params("parallel", "parallel"),
        name=name,
    )(a, w, gain_row)


def _swiglu_up_kernel(h_ref, wg_ref, wu_ref, o_ref, *, hidden):
    h = h_ref[...]
    g = _dot(h, wg_ref[...])
    u = _dot(h, wu_ref[...])
    act = g * jax.nn.sigmoid(g) * u
    bn = act.shape[1]
    col = pl.program_id(1) * bn + lax.broadcasted_iota(jnp.int32, act.shape, 1)
    o_ref[...] = jnp.where(col < hidden, act, 0.0).astype(o_ref.dtype)


def _swiglu_up(h, wg, wu, hidden_padded):
    m, k = h.shape
    hidden = wg.shape[1]
    bm, bn = min(MM_BM, m), SWIGLU_BN
    return pl.pallas_call(
        functools.partial(_swiglu_up_kernel, hidden=hidden),
        out_shape=jax.ShapeDtypeStruct((m, hidden_padded), jnp.bfloat16),
        grid=(m // bm, hidden_padded // bn),
        in_specs=[
            pl.BlockSpec((bm, k), lambda i, j: (i, 0)),
            pl.BlockSpec((k, bn), lambda i, j: (0, j)),
            pl.BlockSpec((k, bn), lambda i, j: (0, j)),
        ],
        out_specs=pl.BlockSpec((bm, bn), lambda i, j: (i, j)),
        compiler_params=_params("parallel", "parallel"),
        name="swiglu_up",
    )(h, wg, wu)


def _mm_acc_res_kernel(a_ref, w_ref, r_ref, o_ref, acc_ref):
    kk = pl.program_id(2)

    @pl.when(kk == 0)
    def _():
        acc_ref[...] = r_ref[...]

    acc_ref[...] += _dot(a_ref[...], w_ref[...])

    @pl.when(kk == pl.num_programs(2) - 1)
    def _():
        o_ref[...] = acc_ref[...]


def _matmul_ktiled_res(a, w, residual, bk, name):
    m, k = a.shape
    n = w.shape[1]
    bm, bn = min(MM_BM, m), min(MM_BN, n)
    return pl.pallas_call(
        _mm_acc_res_kernel,
        out_shape=jax.ShapeDtypeStruct((m, n), jnp.float32),
        grid=(m // bm, n // bn, k // bk),
        in_specs=[
            pl.BlockSpec((bm, bk), lambda i, j, kk: (i, kk)),
            pl.BlockSpec((bk, bn), lambda i, j, kk: (kk, j)),
            pl.BlockSpec((bm, bn), lambda i, j, kk: (i, j)),
        ],
        out_specs=pl.BlockSpec((bm, bn), lambda i, j, kk: (i, j)),
        scratch_shapes=[pltpu.VMEM((bm, bn), jnp.float32)],
        compiler_params=_params("parallel", "parallel", "arbitrary"),
        name=name,
    )(a, w, residual)


def _gdn_gates_kernel(h_ref, w_ref, alog_ref, dt_ref, o_ref):
    nh = alog_ref.shape[0]
    yt = _dot(h_ref[...], w_ref[...]).T
    o_ref[0] = jax.nn.sigmoid(yt[:nh])
    x = yt[nh:2 * nh] + dt_ref[...]
    softplus = jnp.maximum(x, 0.0) + jnp.log1p(jnp.exp(-jnp.abs(x)))
    g = -jnp.exp(alog_ref[...]) * softplus
    lane = lax.broadcasted_iota(jnp.int32, (nh, GDN_CHUNK), 1)
    for c in range(g.shape[1] // GDN_CHUNK):
        acc = g[:, c * GDN_CHUNK:(c + 1) * GDN_CHUNK]
        shift = 1
        while shift < GDN_CHUNK:
            acc = acc + jnp.where(lane >= shift, pltpu.roll(acc, shift, 1), 0.0)
            shift *= 2
        o_ref[1, :, c * GDN_CHUNK:(c + 1) * GDN_CHUNK] = acc


def _gdn_gates(h, w_gate, a_log, dt_bias):
    s, k = h.shape
    nh = a_log.shape[0]
    bm = min(GATES_BM, s)
    w_pad = jnp.pad(_bf(w_gate), ((0, 0), (0, LANE - 2 * nh)))
    out = pl.pallas_call(
        _gdn_gates_kernel,
        out_shape=jax.ShapeDtypeStruct((2, nh, s), jnp.float32),
        grid=(s // bm,),
        in_specs=[
            pl.BlockSpec((bm, k), lambda i: (i, 0)),
            pl.BlockSpec((k, LANE), lambda i: (0, 0)),
            pl.BlockSpec((nh, 1), lambda i: (0, 0)),
            pl.BlockSpec((nh, 1), lambda i: (0, 0)),
        ],
        out_specs=pl.BlockSpec((2, nh, bm), lambda i: (0, 0, i)),
        compiler_params=_params("parallel"),
        name="gdn_gates",
    )(h, w_pad, a_log.reshape(nh, 1), dt_bias.reshape(nh, 1))
    return out.reshape(2, nh, 1, s)


def _gdn_core_kernel(q_ref, k_ref, v_ref, z_ref, cwq_ref, cwk_ref, cwv_ref, gate_ref, gate_prev_ref, onorm_ref, o_ref,
                     state_ref, extq_ref, extk_ref, extv_ref, kq_ref, bo_ref):
    tb = q_ref.shape[0]
    hp = q_ref.shape[1] // HEAD_DIM
    c = GDN_CHUNK
    nt = tb // c
    step = pl.program_id(1)
    slot_w = step % 2
    slot_r = 1 - slot_w

    @pl.when(step == 0)
    def _():
        state_ref[...] = jnp.zeros_like(state_ref)
        for ext in (extq_ref, extk_ref, extv_ref):
            ext[0:SUBLANE, :] = jnp.zeros((SUBLANE, hp * HEAD_DIM), jnp.float32)
        kq_ref[1] = jnp.zeros(kq_ref.shape[1:], kq_ref.dtype)
        bo_ref[1] = jnp.zeros(bo_ref.shape[1:], bo_ref.dtype)

    tiles = [(hh, n) for n in range(nt) for hh in range(hp)]
    state = [state_ref[hh] for hh in range(hp)]
    for n in range(nt):
        sl = slice(n * c, (n + 1) * c)
        for hh in range(hp):
            cols = slice(hh * HEAD_DIM, (hh + 1) * HEAD_DIM)
            ss = _dot(kq_ref[slot_r, hh, n], _bf(state[hh]))
            o = ss[c:] + bo_ref[slot_r, hh, n, c:2 * c, :]
            g_last = jnp.exp(gate_prev_ref[1, hh, :, (n + 1) * c - 1:(n + 1) * c])
            state[hh] = state[hh] * g_last - ss[:c] + bo_ref[slot_r, hh, n, 0:c, :]
            o = o * lax.rsqrt(jnp.mean(o * o, axis=-1, keepdims=True) + EPS) * onorm_ref[...]
            z = z_ref[sl, cols].astype(jnp.float32)
            o_ref[sl, cols] = (o * (z * jax.nn.sigmoid(z))).astype(o_ref.dtype)
    for hh in range(hp):
        state_ref[hh] = state[hh]

    def conv_silu(x_ref, ext, cw_ref):
        ext[SUBLANE:SUBLANE + tb, :] = x_ref[...].astype(jnp.float32)
        acc = ext[pl.ds(SUBLANE, tb), :] * cw_ref[GDN_CONV - 1:GDN_CONV, :]
        for back in range(1, GDN_CONV):
            acc = acc + ext[pl.ds(SUBLANE - back, tb), :] * cw_ref[GDN_CONV - 1 - back:GDN_CONV - back, :]
        ext[0:SUBLANE, :] = ext[tb:tb + SUBLANE, :]
        return acc * jax.nn.sigmoid(acc)

    def l2norm(x):
        return x * lax.rsqrt(jnp.sum(x * x, axis=-1, keepdims=True) + EPS)

    def head_cols(x, hh):
        return x[:, hh * HEAD_DIM:(hh + 1) * HEAD_DIM]

    q_conv = conv_silu(q_ref, extq_ref, cwq_ref)
    k_conv = conv_silu(k_ref, extk_ref, cwk_ref)
    v_conv = conv_silu(v_ref, extv_ref, cwv_ref)
    q_all = [l2norm(head_cols(q_conv, hh)) * (HEAD_DIM ** -0.5) for hh in range(hp)]
    k_all = [l2norm(head_cols(k_conv, hh)) for hh in range(hp)]
    v_all = [head_cols(v_conv, hh) for hh in range(hp)]

    row = lax.broadcasted_iota(jnp.int32, (c, c), 0)
    col = lax.broadcasted_iota(jnp.int32, (c, c), 1)
    causal = row >= col
    strict = row > col
    eye = jnp.where(row == col, 1.0, 0.0)
    pair_masks = [((row >> (l + 1)) == (col >> (l + 1))) & ((row >> l) != (col >> l)) for l in range(c.bit_length() - 1)]

    q_t, k_t, v_t, kb_t, decay_t, egc_t, beta_t, kdec_t, a_t = {}, {}, {}, {}, {}, {}, {}, {}, {}
    for t in tiles:
        hh, n = t
        sl = slice(n * c, (n + 1) * c)
        q_t[t], k_t[t], v_t[t] = q_all[hh][sl], k_all[hh][sl], v_all[hh][sl]
        gc_row = gate_ref[1, hh, :, sl]
        gc_r = jnp.broadcast_to(gc_row, (c, c))
        gc_c = gc_r.T
        beta_t[t] = jnp.broadcast_to(gate_ref[0, hh, :, sl], (c, c)).T
        decay_t[t] = jnp.where(causal, jnp.exp(jnp.where(causal, gc_c - gc_r, 0.0)), 0.0)
        egc_t[t] = jnp.exp(gc_c)
        kdec_t[t] = k_t[t] * jnp.exp(gc_row[:, c - 1:c] - gc_c)
        kb_t[t] = k_t[t] * beta_t[t]
        a_t[t] = jnp.where(strict, _dot_nt(_bf(kb_t[t]), _bf(k_t[t])) * decay_t[t], 0.0)
    t_t = {t: eye - jnp.where(pair_masks[0], a_t[t], 0.0) for t in tiles}
    a16 = {t: _bf(a_t[t]) for t in tiles}
    for off_mask in pair_masks[1:]:
        t16 = {t: _bf(t_t[t]) for t in tiles}
        ta = {t: _dot(t16[t], a16[t]) for t in tiles}
        t_t = {t: jnp.where(off_mask, t_t[t] - _dot(_bf(ta[t]), t16[t]), t_t[t]) for t in tiles}
    wu16 = {t: _bf(_dot(_bf(t_t[t]), _bf(jnp.concatenate([kb_t[t] * egc_t[t], v_t[t] * beta_t[t]], axis=1))))
            for t in tiles}
    attn16 = {t: _bf(jnp.where(causal, _dot_nt(_bf(q_t[t]), _bf(k_t[t])) * decay_t[t], 0.0)) for t in tiles}
    kw_ku = {t: _dot(_bf(kdec_t[t].T), wu16[t]) for t in tiles}
    aw_au = {t: _dot(attn16[t], wu16[t]) for t in tiles}
    for t in tiles:
        hh, n = t
        kq_ref[slot_w, hh, n, 0:c, :] = _bf(kw_ku[t][:, :HEAD_DIM])
        kq_ref[slot_w, hh, n, c:2 * c, :] = _bf(q_t[t] * egc_t[t] - aw_au[t][:, :HEAD_DIM])
        bo_ref[slot_w, hh, n, 0:c, :] = kw_ku[t][:, HEAD_DIM:]
        bo_ref[slot_w, hh, n, c:2 * c, :] = aw_au[t][:, HEAD_DIM:]


def _gdn_core(proj, conv_w, gates, out_norm, heads):
    s = proj.shape[0]
    tb = min(GDN_TB, s)
    d = HEAD_DIM
    hp = GDN_HEADS_PER_STEP
    assert heads % hp == 0
    n_hb = heads // hp
    n_tb = s // tb
    cur = lambda t: jnp.minimum(t, n_tb - 1)
    prev = lambda t: jnp.maximum(t - 1, 0)
    col = lambda off: (lambda h, t: (cur(t), off * n_hb + h))
    cw = lambda off: (lambda h, t: (0, off * n_hb + h))
    nt = tb // GDN_CHUNK
    return pl.pallas_call(
        _gdn_core_kernel,
        out_shape=jax.ShapeDtypeStruct((s, heads * d), jnp.bfloat16),
        grid=(n_hb, n_tb + 1),
        in_specs=[
            pl.BlockSpec((tb, hp * d), col(0)), pl.BlockSpec((tb, hp * d), col(1)), pl.BlockSpec((tb, hp * d), col(2)),
            pl.BlockSpec((tb, hp * d), lambda h, t: (prev(t), 3 * n_hb + h)),
            pl.BlockSpec((GDN_CONV, hp * d), cw(0)), pl.BlockSpec((GDN_CONV, hp * d), cw(1)),
            pl.BlockSpec((GDN_CONV, hp * d), cw(2)),
            pl.BlockSpec((2, hp, 1, tb), lambda h, t: (0, h, 0, cur(t))),
            pl.BlockSpec((2, hp, 1, tb), lambda h, t: (0, h, 0, prev(t))),
            pl.BlockSpec((1, d), lambda h, t: (0, 0)),
        ],
        out_specs=pl.BlockSpec((tb, hp * d), lambda h, t: (prev(t), h)),
        scratch_shapes=[pltpu.VMEM((hp, d, d), jnp.float32)] + [pltpu.VMEM((tb + SUBLANE, hp * d), jnp.float32)] * 3
        + [pltpu.VMEM((2, hp, nt, 2 * GDN_CHUNK, d), jnp.bfloat16), pltpu.VMEM((2, hp, nt, 2 * GDN_CHUNK, d), jnp.float32)],
        compiler_params=_params("parallel", "arbitrary"),
        name="gdn_core",
    )(proj, proj, proj, proj, conv_w, conv_w, conv_w, gates, gates, out_norm.reshape(1, d))


def _rmsnorm_rm_kernel(x_ref, w_ref, o_ref):
    x = x_ref[...]
    y = _bf(x * lax.rsqrt(jnp.mean(x * x, axis=-1, keepdims=True) + EPS) * w_ref[...])
    n = x.shape[0]
    bi = n // ATT_RES
    row = lax.broadcasted_iota(jnp.int32, (n, n), 0)
    col = lax.broadcasted_iota(jnp.int32, (n, n), 1)
    perm = _bf(jnp.where(col == ATT_RES * (row % bi) + row // bi, 1.0, 0.0))
    out = _dot(perm, y)
    for r in range(ATT_RES):
        o_ref[r] = out[r * bi:(r + 1) * bi].astype(o_ref.dtype)


def _rmsnorm_residue_major(x, w):
    s, d = x.shape
    n_sb = s // ATT_SB
    bi = ATT_NORM_BI
    steps = DIL_BLOCK // bi
    out = pl.pallas_call(
        _rmsnorm_rm_kernel,
        out_shape=jax.ShapeDtypeStruct((n_sb, ATT_RES, DIL_BLOCK, d), jnp.bfloat16),
        grid=(n_sb, steps),
        in_specs=[pl.BlockSpec((bi * ATT_RES, d), lambda n, i: (n * steps + i, 0)),
                  pl.BlockSpec((1, d), lambda n, i: (0, 0))],
        out_specs=pl.BlockSpec((None, ATT_RES, bi, d), lambda n, i: (n, 0, i, 0)),
        compiler_params=_params("parallel", "parallel"),
        name="rmsnorm_residue_major",
    )(x, w.reshape(1, d))
    return out.reshape(s, d)


def _mm_res_natural_kernel(a_ref, w_ref, r_ref, o_ref, anat_ref):
    @pl.when(pl.program_id(1) == 0)
    def _():
        bi = ATT_NORM_BI
        n = bi * ATT_RES
        row = lax.broadcasted_iota(jnp.int32, (n, n), 0)
        col = lax.broadcasted_iota(jnp.int32, (n, n), 1)
        perm = _bf(jnp.where(col == bi * (row % ATT_RES) + row // ATT_RES, 1.0, 0.0))
        for grp in range(DIL_BLOCK // bi):
            gathered = jnp.concatenate(
                [a_ref[r * DIL_BLOCK + grp * bi:r * DIL_BLOCK + (grp + 1) * bi, :] for r in range(ATT_RES)], axis=0)
            anat_ref[grp * n:(grp + 1) * n, :] = _dot(perm, gathered).astype(anat_ref.dtype)

    o_ref[...] = r_ref[...] + _dot(anat_ref[...], w_ref[...])


def _matmul_res_from_residue_major(a, w, residual, name):
    m, k = a.shape
    n = w.shape[1]
    bn = min(ATT_OUT_BN, n)
    return pl.pallas_call(
        _mm_res_natural_kernel,
        out_shape=jax.ShapeDtypeStruct((m, n), jnp.float32),
        grid=(m // ATT_SB, n // bn),
        in_specs=[pl.BlockSpec((ATT_SB, k), lambda i, j: (i, 0)), pl.BlockSpec((k, bn), lambda i, j: (0, j)),
                  pl.BlockSpec((ATT_SB, bn), lambda i, j: (i, j))],
        out_specs=pl.BlockSpec((ATT_SB, bn), lambda i, j: (i, j)),
        scratch_shapes=[pltpu.VMEM((ATT_SB, k), a.dtype)],
        compiler_params=_params("parallel", "arbitrary"),
        name=name,
    )(a, w, residual)


def _dil_attn_kernel(*refs, groups, heads_per_step):
    ng = len(groups)
    hs = heads_per_step
    q_refs = refs[:ng]
    kv_refs = refs[ng:5 * ng]
    o_ref = refs[5 * ng]
    og_refs = refs[5 * ng + 1:6 * ng + 1]
    lse_refs = refs[6 * ng + 1:7 * ng + 1]
    blk = DIL_BLOCK
    not_first = pl.program_id(0) > 0
    row = lax.broadcasted_iota(jnp.int32, (hs * blk, 2 * blk), 0) % blk
    col = lax.broadcasted_iota(jnp.int32, (hs * blk, 2 * blk), 1)
    is_prev = col < blk
    col = col % blk
    ones16 = jnp.ones((2 * blk, HEAD_DIM), jnp.bfloat16)
    neg_inf = jnp.float32(-jnp.inf)

    for gi, (window, dil) in enumerate(groups):
        span = window // dil
        per = ATT_RES // dil
        chunk = blk // per
        jq = per * (row % chunk) + row // chunk
        jk = per * (col % chunk) + col // chunk
        own_ok = (~is_prev) & (jq - jk >= 0) & (jq - jk <= span)
        prev_ok = is_prev & (jq + blk - jk <= span)
        mask = own_ok | prev_ok
        mask_first = own_ok | (prev_ok & not_first)
        k_own, k_prev, v_own, v_prev = kv_refs[4 * gi:4 * gi + 4]

        def tile_rows(ref, r_d, lo, cols=slice(None)):
            return jnp.concatenate([ref[r_d + dil * m, lo:lo + chunk, cols] for m in range(per)], axis=0)

        tiles = [(r_d, b) for r_d in range(dil) for b in range(per)]
        for t0 in range(0, len(tiles), ATT_TILE_BATCH):
            batch = tiles[t0:t0 + ATT_TILE_BATCH]
            q16, k16, v16 = {}, {}, {}
            for t in batch:
                r_d, b = t
                lo = chunk * b
                q16[t] = _bf(jnp.concatenate(
                    [tile_rows(q_refs[gi], r_d, lo, slice(h * HEAD_DIM, (h + 1) * HEAD_DIM)) for h in range(hs)], axis=0))
                if b > 0:
                    kp, vp = tile_rows(k_own, r_d, lo - chunk), tile_rows(v_own, r_d, lo - chunk)
                else:
                    kp, vp = tile_rows(k_prev, r_d, 0), tile_rows(v_prev, r_d, 0)
                k16[t] = _bf(jnp.concatenate([kp, tile_rows(k_own, r_d, lo)], axis=0))
                v16[t] = jnp.concatenate([_bf(jnp.concatenate([vp, tile_rows(v_own, r_d, lo)], axis=0)), ones16], axis=1)
            s = {t: jnp.where(mask if t[1] > 0 else mask_first, _dot_nt(q16[t], k16[t]), neg_inf) for t in batch}
            mx = {t: jnp.max(jnp.maximum(s[t][:, :blk], s[t][:, blk:]), axis=-1, keepdims=True) for t in batch}
            e16 = {t: _bf(jnp.exp(s[t] - mx[t])) for t in batch}
            pv = {t: _dot(e16[t], v16[t]) for t in batch}
            for t in batch:
                r_d, b = t
                den = pv[t][:, HEAD_DIM:]
                o_t = pv[t][:, :HEAD_DIM] / den
                lse_t = mx[t] + jnp.log(den)
                for h in range(hs):
                    for m in range(per):
                        src = slice(h * blk + m * chunk, h * blk + (m + 1) * chunk)
                        og_refs[gi][h, r_d + dil * m, chunk * b:chunk * (b + 1), :] = o_t[src]
                        lse_refs[gi][h, r_d + dil * m, chunk * b:chunk * (b + 1), :] = lse_t[src]

    for h in range(hs):
        lse_max = lse_refs[0][h]
        for gi in range(1, ng):
            lse_max = jnp.maximum(lse_max, lse_refs[gi][h])
        num = jnp.zeros(lse_max.shape, jnp.float32)
        den = jnp.zeros(lse_max.shape, jnp.float32)
        for gi in range(ng):
            wgt = jnp.exp(lse_refs[gi][h] - lse_max)
            num = num + wgt * og_refs[gi][h]
            den = den + wgt
        o_ref[:, :, h * HEAD_DIM:(h + 1) * HEAD_DIM] = (num / den).astype(o_ref.dtype)


def _dilated_attention(q, kv, heads, kv_heads):
    s = q.shape[0]
    ng = len(DIL_GROUPS)
    res, blk, d, hs = ATT_RES, DIL_BLOCK, HEAD_DIM, ATT_HEADS_PER_STEP
    assert s % ATT_SB == 0 and all(w // dil <= blk and res % dil == 0 for w, dil in DIL_GROUPS)
    n_sb = s // ATT_SB
    rep = heads // kv_heads
    assert rep % hs == 0
    q4 = q.reshape(n_sb, res, blk, q.shape[1])
    kv4 = kv.reshape(n_sb, res, blk, kv.shape[1])
    in_specs = [pl.BlockSpec((None, res, blk, hs * d),
                             lambda n, g, r, gi=gi: (n, 0, 0, (gi * heads + g * rep) // hs + r)) for gi in range(ng)]
    args = [q4] * ng
    for gi, (_, dil) in enumerate(DIL_GROUPS):
        chunk = blk // (res // dil)
        for part in range(2):
            cb = (gi * 2 + part) * kv_heads
            in_specs.append(pl.BlockSpec((None, res, blk, d), lambda n, g, r, cb=cb: (n, 0, 0, cb + g)))
            in_specs.append(pl.BlockSpec((None, res, chunk, d),
                                         lambda n, g, r, cb=cb, last=blk // chunk - 1: (jnp.maximum(n - 1, 0), 0, last, cb + g)))
            args += [kv4, kv4]
    out = pl.pallas_call(
        functools.partial(_dil_attn_kernel, groups=DIL_GROUPS, heads_per_step=hs),
        out_shape=jax.ShapeDtypeStruct((n_sb, res, blk, heads * d), jnp.bfloat16),
        grid=(n_sb, kv_heads, rep // hs),
        in_specs=in_specs,
        out_specs=pl.BlockSpec((None, res, blk, hs * d), lambda n, g, r: (n, 0, 0, (g * rep) // hs + r)),
        scratch_shapes=[pltpu.VMEM((hs, res, blk, d), jnp.float32)] * (2 * ng),
        compiler_params=_params("parallel", "parallel", "arbitrary"),
        name="dilated_attention",
    )(*args)
    return out.reshape(s, heads * d)


def _gdn_layer(x, attn_norm, w_in, conv_w, a_log, dt_bias, out_norm, w_out):
    heads = a_log.shape[0]
    qk = heads * HEAD_DIM
    h = _rmsnorm(x, attn_norm)
    proj = _matmul(h, _bf(w_in[:, :4 * qk]), out_dtype=jnp.bfloat16, name="gdn_in_proj")
    gates = _gdn_gates(h, w_in[:, 4 * qk:], a_log, dt_bias)
    o = _gdn_core(proj, conv_w, gates, out_norm, heads)
    return _matmul(o, _bf(w_out), residual=x, name="gdn_out_proj")


def _shared_kv(x, kv_norm, w_kv, k_norm):
    n_dil = k_norm.shape[0]
    kvw = w_kv.shape[1] // (2 * n_dil)
    h = _rmsnorm_residue_major(x, kv_norm)
    gain = jnp.concatenate([jnp.tile(k_norm, (1, kvw // HEAD_DIM)), jnp.ones((n_dil, kvw), jnp.float32)], axis=1).reshape(1, -1)
    return _matmul_headnorm(h, _bf(w_kv), gain, scale=1.0, bn=kvw, norm_every=2, name="kv_proj")


def _dilated_layer(x, kv, attn_norm, w_q, q_norm, w_out):
    n_dil = q_norm.shape[0]
    qw = w_q.shape[1] // n_dil
    kvw = kv.shape[1] // (2 * n_dil)
    heads, kv_heads = qw // HEAD_DIM, kvw // HEAD_DIM
    h = _rmsnorm_residue_major(x, attn_norm)
    gain = jnp.tile(q_norm, (1, heads)).reshape(1, -1)
    q = _matmul_headnorm(h, _bf(w_q), gain, scale=HEAD_DIM ** -0.5, bn=min(MM_BN, qw), norm_every=1, name="q_proj")
    o = _dilated_attention(q, kv, heads, kv_heads)
    return _matmul_res_from_residue_major(o, _bf(w_out), x, name="attn_out_proj")


def _ffn_layer(x, norm_w, w_gate_up, w_down):
    hidden = w_down.shape[0]
    hidden_padded = -(-hidden // MM_BN) * MM_BN
    h = _rmsnorm(x, norm_w)
    act = _swiglu_up(h, _bf(w_gate_up[:, :hidden]), _bf(w_gate_up[:, hidden:]), hidden_padded)
    w_down_p = jnp.pad(_bf(w_down), ((0, hidden_padded - hidden), (0, 0)))
    return _matmul_ktiled_res(act, w_down_p, x, _pick_down_bk(hidden_padded), name="ffn_down")


def kernel(x, a_attn_norm, a_w_in, a_conv_w, a_a_log, a_dt_bias, a_out_norm, a_w_out, kv_norm, w_kv, k_norm,
           b_attn_norm, b_w_q, b_q_norm, b_w_out, ffn_norm, ffn_w_gate_up, ffn_w_down):
    batch, s, d = x.shape
    assert batch == 1
    x = x.reshape(s, d)
    depth = ffn_norm.shape[0]
    n_a = a_attn_norm.shape[0]
    kv = None
    for layer in range(depth):
        if layer < n_a:
            i = layer
            x = _gdn_layer(x, a_attn_norm[i], a_w_in[i], a_conv_w[i], a_a_log[i], a_dt_bias[i], a_out_norm[i], a_w_out[i])
        else:
            if layer == n_a:
                kv = _shared_kv(x, kv_norm, w_kv, k_norm)
            j = layer - n_a
            x = _dilated_layer(x, kv, b_attn_norm[j], b_w_q[j], b_q_norm[j], b_w_out[j])
        x = _ffn_layer(x, ffn_norm[layer], ffn_w_gate_up[layer], ffn_w_down[layer])
    return x.reshape(batch, s, d)
```
